```python
import math
import jax, jax.numpy as jnp
from jax import lax
import numpy as np


D_MODEL = 1024
BATCH = 16
SEQ = 2048
DEPTH = 2

GRID_W = 64
CTX_LEN = 256
N_MIXERS = 2
N_ATT_LAYERS = (DEPTH + 1) // 2
N_SSM_LAYERS = DEPTH // 2

ATT_HEADS = 16
ATT_KV_HEADS = 4
HEAD_DIM = 64
ATT_GROUP = ATT_HEADS // ATT_KV_HEADS
ATT_DIM = ATT_HEADS * HEAD_DIM
ATT_IN_DIM = ATT_DIM + 2 * ATT_KV_HEADS * HEAD_DIM
WINDOW = 128
BLOCK = 128
SPAN = BLOCK + 2 * WINDOW
ROPE_BASE = 10000.0
ROPE_AXIS_DIM = HEAD_DIM // 2

SSM_D_INNER = 2 * D_MODEL
SSM_HEAD_DIM = 64
SSM_HEADS = SSM_D_INNER // SSM_HEAD_DIM
SSM_GROUPS = 4
SSM_GROUP_HEADS = SSM_HEADS // SSM_GROUPS
SSM_STATE = 128
SSM_CONV_W = 5
SSM_CHUNK = 128
SSM_CONV_DIM = SSM_D_INNER + 2 * SSM_GROUPS * SSM_STATE
SSM_IN_DIM = SSM_D_INNER + SSM_CONV_DIM + 2 * SSM_HEADS
SSM_NORM_GROUPS = SSM_GROUPS

FF_DIM = 4 * D_MODEL
N_MOD = 6
ALPHA = (2.0 * DEPTH) ** 0.25
BETA = (8.0 * DEPTH) ** -0.25
LN_EPS = 1e-5
RMS_EPS = 1e-5
NEG_INF = -1e30

kernel_name = 'hybrid_swa_ssd_flow_backbone'


def layer_norm(x, g, b):
    xf = x.astype(jnp.float32)
    mu = jnp.mean(xf, axis=-1, keepdims=True)
    var = jnp.mean(jnp.square(xf - mu), axis=-1, keepdims=True)
    y = (xf - mu) * lax.rsqrt(var + LN_EPS) * g.astype(jnp.float32) + b.astype(jnp.float32)
    return y.astype(x.dtype)


def modulate(h, shift, scale):
    return h * (1.0 + scale) + shift


def squared_relu_mlp(h, w1, w2):
    return jnp.square(jax.nn.relu(h @ w1)) @ w2


def axial_rope_tables(rows):
    row = jnp.repeat(jnp.arange(rows), GRID_W).astype(jnp.float32)
    col = jnp.tile(jnp.arange(GRID_W), rows).astype(jnp.float32)
    inv_freq = ROPE_BASE ** (-jnp.arange(0, ROPE_AXIS_DIM, 2, dtype=jnp.float32) / ROPE_AXIS_DIM)
    ang_r = row[:, None] * inv_freq[None, :]
    ang_c = col[:, None] * inv_freq[None, :]
    return (jnp.cos(ang_r), jnp.sin(ang_r), jnp.cos(ang_c), jnp.sin(ang_c))


def rotate_half(x, cos, sin):
    x1, x2 = jnp.split(x, 2, axis=-1)
    return jnp.concatenate([x1 * cos - x2 * sin, x2 * cos + x1 * sin], axis=-1)


def apply_axial_rope(x, tables):
    cos_r, sin_r, cos_c, sin_c = tables
    bshape = (x.shape[1],) + (1,) * (x.ndim - 3) + (-1,)
    xf = x.astype(jnp.float32)
    xr = rotate_half(xf[..., :ROPE_AXIS_DIM], cos_r.reshape(bshape), sin_r.reshape(bshape))
    xc = rotate_half(xf[..., ROPE_AXIS_DIM:], cos_c.reshape(bshape), sin_c.reshape(bshape))
    return jnp.concatenate([xr, xc], axis=-1).astype(x.dtype)


def attention_mixer(u_lat, u_ctx, w_in, w_out, sink, rope, need_ctx_out):
    bsz, n, _ = u_lat.shape
    n_ctx = u_ctx.shape[1]
    nb = n // BLOCK
    scale = HEAD_DIM ** -0.5
    sink_f = sink.astype(jnp.float32).reshape(ATT_KV_HEADS, ATT_GROUP)

    def project(u):
        length = u.shape[1]
        q, k, v = jnp.split(u @ w_in, [ATT_DIM, ATT_DIM + ATT_KV_HEADS * HEAD_DIM], axis=-1)
        return (q.reshape(bsz, length, ATT_KV_HEADS, ATT_GROUP, HEAD_DIM),
                k.reshape(bsz, length, ATT_KV_HEADS, HEAD_DIM),
                v.reshape(bsz, length, ATT_KV_HEADS, HEAD_DIM))

    def sink_softmax(logits):
        s = jnp.broadcast_to(sink_f[None, :, :, None, None], logits.shape[:-1] + (1,))
        return jax.nn.softmax(jnp.concatenate([logits, s], axis=-1), axis=-1)

    q_l, k_l, v_l = project(u_lat)
    q_c, k_c, v_c = project(u_ctx)
    qr_l = apply_axial_rope(q_l, rope)
    kr_l = apply_axial_rope(k_l, rope)
    pad = ((0, 0), (WINDOW, WINDOW), (0, 0), (0, 0))
    k_pad = jnp.pad(kr_l, pad)
    v_pad = jnp.pad(v_l, pad)
    q_blocks = jnp.moveaxis(qr_l.reshape(bsz, nb, BLOCK, ATT_KV_HEADS, ATT_GROUP, HEAD_DIM), 1, 0)
    qp_blocks = jnp.moveaxis(q_l.reshape(bsz, nb, BLOCK, ATT_KV_HEADS, ATT_GROUP, HEAD_DIM), 1, 0)
    offs_q = jnp.arange(BLOCK)
    offs_k = jnp.arange(SPAN) - WINDOW

    def block_attend(args):
        blk, qb, qpb = args
        start = blk * BLOCK
        kw = lax.dynamic_slice_in_dim(k_pad, start, SPAN, axis=1)
        vw = lax.dynamic_slice_in_dim(v_pad, start, SPAN, axis=1)
        qi = start + offs_q
        kj = start + offs_k
        ok = (kj[None, :] >= 0) & (kj[None, :] < n) & (jnp.abs(qi[:, None] - kj[None, :]) <= WINDOW)
        s_win = jnp.einsum('blgrd,bsgd->bgrls', qb, kw).astype(jnp.float32) * scale
        s_win = jnp.where(ok, s_win, NEG_INF)
        s_ctx = jnp.einsum('blgrd,bsgd->bgrls', qpb, k_c).astype(jnp.float32) * scale
        p = sink_softmax(jnp.concatenate([s_win, s_ctx], axis=-1)).astype(vw.dtype)
        return (jnp.einsum('bgrls,bsgd->blgrd', p[..., :SPAN], vw)
                + jnp.einsum('bgrls,bsgd->blgrd', p[..., SPAN:SPAN + n_ctx], v_c))

    o = lax.map(block_attend, (jnp.arange(nb), q_blocks, qp_blocks))
    y_lat = jnp.moveaxis(o, 0, 1).reshape(bsz, n, ATT_DIM) @ w_out
    y_ctx = None
    if need_ctx_out:
        s = jnp.einsum('blgrd,bsgd->bgrls', q_c, k_c).astype(jnp.float32) * scale
        p = sink_softmax(s).astype(v_c.dtype)
        o_c = jnp.einsum('bgrls,bsgd->blgrd', p[..., :n_ctx], v_c)
        y_ctx = o_c.reshape(bsz, n_ctx, ATT_DIM) @ w_out
    return y_lat, y_ctx


def depthwise_conv(x, w, bias):
    ch = x.shape[-1]
    y = lax.conv_general_dilated(x, w[:, None, :].astype(x.dtype), window_strides=(1,),
                                 padding=[(SSM_CONV_W // 2, SSM_CONV_W // 2)],
                                 dimension_numbers=('NWC', 'WIO', 'NWC'), feature_group_count=ch)
    return y + bias


def segsum(a):
    t = a.shape[-1]
    a_rep = jnp.broadcast_to(a[..., None], a.shape + (t,))
    a_rep = jnp.where(jnp.tril(jnp.ones((t, t), dtype=bool), -1), a_rep, 0.0)
    cs = jnp.cumsum(a_rep, axis=-2)
    return jnp.where(jnp.tril(jnp.ones((t, t), dtype=bool)), cs, -jnp.inf)


def ssd_scan(xs, dt, a_neg, bm, cm, init_state, with_outputs):
    b, length, g, r, p = xs.shape
    n = bm.shape[-1]
    nc = length // SSM_CHUNK
    q = SSM_CHUNK
    dt4 = dt.reshape(b, length, g, r)
    x_dt = (xs.astype(jnp.float32) * dt4[..., None]).reshape(b, nc, q, g, r, p)
    a = jnp.moveaxis((dt4 * a_neg.reshape(g, r)).reshape(b, nc, q, g, r), 2, -1)
    a_cs = jnp.cumsum(a, axis=-1)
    bc = bm.astype(jnp.float32).reshape(b, nc, q, g, n)
    cc = cm.astype(jnp.float32).reshape(b, nc, q, g, n)
    decay_states = jnp.exp(a_cs[..., -1:] - a_cs)
    states = jnp.einsum('bcsgn,bcgrs,bcsgrp->bcgrpn', bc, decay_states, x_dt)
    states = jnp.concatenate([init_state[:, None], states], axis=1)
    chunk_a = jnp.pad(a_cs[..., -1], ((0, 0), (1, 0), (0, 0), (0, 0)))
    decay_chunk = jnp.exp(segsum(jnp.moveaxis(chunk_a, 1, -1)))
    new_states = jnp.einsum('bgrzc,bcgrpn->bzgrpn', decay_chunk, states)
    final_state = new_states[:, -1]
    if not with_outputs:
        return None, final_state
    prev_states = new_states[:, :-1]
    lmat = jnp.exp(segsum(a))
    cb = jnp.einsum('bclgn,bcsgn->bcgls', cc, bc)
    y_diag = jnp.einsum('bcgrls,bcsgrp->bclgrp', cb[:, :, :, None] * lmat, x_dt)
    y_off = jnp.einsum('bclgn,bcgrpn,bcgrl->bclgrp', cc, prev_states, jnp.exp(a_cs))
    return (y_diag + y_off).reshape(b, length, g, r, p), final_state


def ssm_mixer(u_lat, u_ctx, w_in, conv_w, conv_b, dt_bias, a_log, d_skip, norm_g, w_out, need_ctx_out):
    a_neg = -jnp.exp(a_log.astype(jnp.float32))
    bsz = u_lat.shape[0]

    def flip(t):
        return jnp.flip(t, axis=1)

    def prep(u):
        length = u.shape[1]
        z, xbc, dt = jnp.split(u @ w_in, [SSM_D_INNER, SSM_D_INNER + SSM_CONV_DIM], axis=-1)
        xbc = jax.nn.silu(depthwise_conv(xbc, conv_w, conv_b))
        xs, bm, cm = jnp.split(xbc, [SSM_D_INNER, SSM_D_INNER + SSM_GROUPS * SSM_STATE], axis=-1)
        xs = xs.reshape(bsz, length, SSM_GROUPS, SSM_GROUP_HEADS, SSM_HEAD_DIM)
        bm = bm.reshape(bsz, length, SSM_GROUPS, SSM_STATE)
        cm = cm.reshape(bsz, length, SSM_GROUPS, SSM_STATE)
        dt = jax.nn.softplus(dt.astype(jnp.float32).reshape(bsz, length, 2, SSM_HEADS)
                             + dt_bias.astype(jnp.float32))
        return z, xs, bm, cm, dt

    def finish(z, xs, y_f, y_b):
        length = xs.shape[1]
        y = y_f + y_b + xs.astype(jnp.float32) * d_skip.astype(jnp.float32).reshape(SSM_GROUPS, SSM_GROUP_HEADS, 1)
        y = y.reshape(bsz, length, SSM_D_INNER) * jax.nn.silu(z.astype(jnp.float32))
        y = y.reshape(bsz, length, SSM_NORM_GROUPS, -1)
        y = y * lax.rsqrt(jnp.mean(jnp.square(y), axis=-1, keepdims=True) + RMS_EPS)
        y = (y.reshape(bsz, length, SSM_D_INNER) * norm_g.astype(jnp.float32)).astype(z.dtype)
        return y @ w_out

    z_c, x_c, b_c, c_c, dt_c = prep(u_ctx)
    z_l, x_l, b_l, c_l, dt_l = prep(u_lat)
    init = jnp.zeros((bsz, SSM_GROUPS, SSM_GROUP_HEADS, SSM_HEAD_DIM, SSM_STATE), jnp.float32)
    yc_f, st_f = ssd_scan(x_c, dt_c[:, :, 0], a_neg[0], b_c, c_c, init, need_ctx_out)
    yc_b, st_b = ssd_scan(flip(x_c), flip(dt_c[:, :, 1]), a_neg[1], flip(b_c), flip(c_c), init, need_ctx_out)
    yl_f, _ = ssd_scan(x_l, dt_l[:, :, 0], a_neg[0], b_l, c_l, st_f, True)
    yl_b, _ = ssd_scan(flip(x_l), flip(dt_l[:, :, 1]), a_neg[1], flip(b_l), flip(c_l), st_b, True)
    y_lat = finish(z_l, x_l, yl_f, flip(yl_b))
    y_ctx = finish(z_c, x_c, yc_f, flip(yc_b)) if need_ctx_out else None
    return y_lat, y_ctx


def setup_inputs(seed: int = 0) -> dict:
    key = jax.random.key(seed)
    ks = jax.random.split(key, 24)
    f = jnp.float32
    d = D_MODEL

    def nrm(k, shape, scale):
        return jax.random.normal(k, shape, f) * scale

    dt0 = jnp.exp(jax.random.uniform(ks[15], (N_SSM_LAYERS, 2, SSM_HEADS), f, math.log(1e-3), math.log(1e-1)))
    return {
        'x': nrm(ks[0], (BATCH, SEQ, d), 1.0),
        'c': nrm(ks[1], (BATCH, d), 1.0),
        'ctx': nrm(ks[2], (BATCH, CTX_LEN, d), 1.0),
        'c_ctx': nrm(ks[3], (d,), 1.0),
        'w_mod': nrm(ks[4], (DEPTH, d, N_MOD * d), 0.5 * d ** -0.5),
        'b_mod': nrm(ks[5], (DEPTH, N_MOD * d), 0.02),
        'ln_mix_g': 1.0 + nrm(ks[6], (DEPTH, d), 0.02),
        'ln_mix_b': nrm(ks[7], (DEPTH, d), 0.02),
        'ln_ff_g': 1.0 + nrm(ks[8], (DEPTH, d), 0.02),
        'ln_ff_b': nrm(ks[9], (DEPTH, d), 0.02),
        'att_w_in': nrm(ks[10], (N_ATT_LAYERS, d, ATT_IN_DIM), d ** -0.5),
        'att_w_out': nrm(ks[11], (N_ATT_LAYERS, ATT_DIM, d), BETA * ATT_DIM ** -0.5),
        'att_sink': nrm(ks[12], (N_ATT_LAYERS, ATT_HEADS), 0.5),
        'ssm_w_in': nrm(ks[13], (N_SSM_LAYERS, d, SSM_IN_DIM), d ** -0.5),
        'ssm_conv_w': nrm(ks[14], (N_SSM_LAYERS, SSM_CONV_W, SSM_CONV_DIM), SSM_CONV_W ** -0.5),
        'ssm_conv_b': nrm(ks[16], (N_SSM_LAYERS, SSM_CONV_DIM), 0.02),
        'ssm_dt_bias': dt0 + jnp.log(-jnp.expm1(-dt0)),
        'ssm_a_log': jnp.log(jax.random.uniform(ks[17], (N_SSM_LAYERS, 2, SSM_HEADS), f, 1.0, 16.0)),
        'ssm_d': 1.0 + nrm(ks[18], (N_SSM_LAYERS, SSM_HEADS), 0.1),
        'ssm_norm_g': 1.0 + nrm(ks[19], (N_SSM_LAYERS, SSM_D_INNER), 0.02),
        'ssm_w_out': nrm(ks[20], (N_SSM_LAYERS, SSM_D_INNER, d), BETA * SSM_D_INNER ** -0.5),
        'ff_w1': nrm(ks[21], (DEPTH, d, FF_DIM), d ** -0.5),
        'ff_w2': nrm(ks[22], (DEPTH, FF_DIM, d), BETA * FF_DIM ** -0.5),
    }


def reference(x, c, ctx, c_ctx, w_mod, b_mod, ln_mix_g, ln_mix_b, ln_ff_g, ln_ff_b,
              att_w_in, att_w_out, att_sink, ssm_w_in, ssm_conv_w, ssm_conv_b, ssm_dt_bias,
              ssm_a_log, ssm_d, ssm_norm_g, ssm_w_out, ff_w1, ff_w2):
    rows = x.shape[1] // GRID_W
    rope = axial_rope_tables(rows)
    h_lat, h_ctx = x, ctx
    for i in range(DEPTH):
        last = i == DEPTH - 1
        j = i // N_MIXERS
        m_lat = (jax.nn.silu(c) @ w_mod[i] + b_mod[i])[:, None, :]
        m_ctx = (jax.nn.silu(c_ctx) @ w_mod[i] + b_mod[i])[None, None, :]
        sh_m_l, sc_m_l, g_m_l, sh_f_l, sc_f_l, g_f_l = jnp.split(m_lat, N_MOD, axis=-1)
        sh_m_c, sc_m_c, g_m_c, sh_f_c, sc_f_c, g_f_c = jnp.split(m_ctx, N_MOD, axis=-1)
        u_lat = modulate(h_lat, sh_m_l, sc_m_l)
        u_ctx = modulate(h_ctx, sh_m_c, sc_m_c)
        if i % N_MIXERS == 0:
            y_lat, y_ctx = attention_mixer(u_lat, u_ctx, att_w_in[j], att_w_out[j], att_sink[j], rope, not last)
        else:
            y_lat, y_ctx = ssm_mixer(u_lat, u_ctx, ssm_w_in[j], ssm_conv_w[j], ssm_conv_b[j], ssm_dt_bias[j],
                                     ssm_a_log[j], ssm_d[j], ssm_norm_g[j], ssm_w_out[j], not last)
        h_lat = layer_norm(ALPHA * h_lat + g_m_l * y_lat, ln_mix_g[i], ln_mix_b[i])
        f_lat = squared_relu_mlp(modulate(h_lat, sh_f_l, sc_f_l), ff_w1[i], ff_w2[i])
        h_lat = layer_norm(ALPHA * h_lat + g_f_l * f_lat, ln_ff_g[i], ln_ff_b[i])
        if not last:
            h_ctx = layer_norm(ALPHA * h_ctx + g_m_c * y_ctx, ln_mix_g[i], ln_mix_b[i])
            f_ctx = squared_relu_mlp(modulate(h_ctx, sh_f_c, sc_f_c), ff_w1[i], ff_w2[i])
            h_ctx = layer_norm(ALPHA * h_ctx + g_f_c * f_ctx, ln_ff_g[i], ln_ff_b[i])
    return h_lat
```

```python
import functools

import jax
import jax.numpy as jnp
from jax import lax
from jax.experimental import pallas as pl
from jax.experimental.pallas import tpu as pltpu

F32 = jnp.float32
BF16 = jnp.bfloat16

GRID_W = 64
N_MIXERS = 2
ATT_HEADS = 16
ATT_KV_HEADS = 4
HEAD_DIM = 64
ATT_GROUP = ATT_HEADS // ATT_KV_HEADS
WINDOW = 128
BLOCK = 128
ROPE_BASE = 10000.0
ROPE_AXIS_DIM = HEAD_DIM // 2

SSM_HEAD_DIM = 64
SSM_GROUPS = 4
SSM_STATE = 128
SSM_CONV_W = 5
SSM_CHUNK = 128

N_MOD = 6
LN_EPS = 1e-5
RMS_EPS = 1e-5
NEG_INF = -1e30

LANES = 128
BF16_SUBLANES = 16
MOD_ROWS_ALIGN = 16
VMEM_LIMIT = 48 * 1024 * 1024


def _cparams(sem):
    return pltpu.CompilerParams(dimension_semantics=sem, vmem_limit_bytes=VMEM_LIMIT)


def _sigmoid(x):
    return 1.0 / (1.0 + jnp.exp(-x))


def _dot(a, b):
    return jnp.dot(a, b, preferred_element_type=F32)


def _dot_nt(a, b):
    return lax.dot_general(a, b, (((1,), (1,)), ((), ())), preferred_element_type=F32)


def _dot_tn(a, b):
    return lax.dot_general(a, b, (((0,), (0,)), ((), ())), preferred_element_type=F32)


def _layer_norm_rows(v, g, b):
    mu = jnp.mean(v, axis=-1, keepdims=True)
    d = v - mu
    var = jnp.mean(d * d, axis=-1, keepdims=True)
    return d * lax.rsqrt(var + LN_EPS) * g + b


def _mod_kernel(c_ref, w_ref, b_ref, o_ref):
    c = c_ref[...]
    s = (c * _sigmoid(c)).astype(BF16)
    o_ref[0] = _dot(s, w_ref[0].astype(BF16)) + b_ref[0]


def _mod_call(cc, w_mod, b_mod):
    depth, d, n = w_mod.shape
    rows = cc.shape[0]
    tn = 1024
    return pl.pallas_call(
        _mod_kernel,
        grid=(depth, n // tn),
        in_specs=[
            pl.BlockSpec((rows, d), lambda l, j: (0, 0)),
            pl.BlockSpec((1, d, tn), lambda l, j: (l, 0, j)),
            pl.BlockSpec((1, 1, tn), lambda l, j: (l, 0, j)),
        ],
        out_specs=pl.BlockSpec((1, rows, tn), lambda l, j: (l, 0, j)),
        out_shape=jax.ShapeDtypeStruct((depth, rows, n), F32),
        compiler_params=_cparams(("arbitrary", "arbitrary")),
        name="adaln_mod",
    )(cc, w_mod, b_mod.reshape(depth, 1, n))


def _mod_index(which, tm, seq, ctx_row):
    if ctx_row is None:
        return lambda i, *_: ((i * tm // seq) * N_MOD + which, 0, 0)
    return lambda i, *_: (ctx_row * N_MOD + which, 0, 0)


def _att_inproj_kernel(*refs, rope, nq):
    if rope:
        x_ref, sh_ref, sc_ref, w_ref, cos_ref, sin_ref, qkv_ref, rot_ref = refs
    else:
        x_ref, sh_ref, sc_ref, w_ref, qkv_ref = refs
    u = (x_ref[...] * (1.0 + sc_ref[0]) + sh_ref[0]).astype(BF16)
    acc = _dot(u, w_ref[...])
    n = acc.shape[1]
    nrot = rot_ref.shape[1] if rope else 0
    if rope:
        cos = cos_ref[...]
        sin = sin_ref[...]
        lane = lax.broadcasted_iota(jnp.int32, cos.shape, 1)
        first_half = (lane & (ROPE_AXIS_DIM - 1)) < (ROPE_AXIS_DIM // 2)
    for cidx in range(n // LANES):
        chunk = acc[:, cidx * LANES:(cidx + 1) * LANES]
        if cidx * LANES < nq:
            chunk = chunk * (HEAD_DIM ** -0.5)
        qkv_ref[:, cidx * LANES:(cidx + 1) * LANES] = chunk.astype(BF16)
        if rope and cidx * LANES < nrot:
            half = ROPE_AXIS_DIM // 2
            swapped = jnp.where(first_half, pltpu.roll(chunk, LANES - half, 1), pltpu.roll(chunk, half, 1))
            rot_ref[:, cidx * LANES:(cidx + 1) * LANES] = (chunk * cos + swapped * sin).astype(BF16)


def _att_inproj_call(h, modt, w, tables, *, seq, ctx_row, nq, nrot, tm):
    t, d = h.shape
    n = w.shape[1]
    rope = tables is not None
    in_specs = [
        pl.BlockSpec((tm, d), lambda i: (i, 0)),
        pl.BlockSpec((1, 1, d), _mod_index(0, tm, seq, ctx_row)),
        pl.BlockSpec((1, 1, d), _mod_index(1, tm, seq, ctx_row)),
        pl.BlockSpec((d, n), lambda i: (0, 0)),
    ]
    args = [h, modt, modt, w]
    out_specs = [pl.BlockSpec((tm, n), lambda i: (i, 0))]
    out_shape = [jax.ShapeDtypeStruct((t, n), BF16)]
    if rope:
        per_seq = seq // tm
        in_specs += [pl.BlockSpec((tm, LANES), lambda i: (i % per_seq, 0))] * 2
        args += list(tables)
        out_specs.append(pl.BlockSpec((tm, nrot), lambda i: (i, 0)))
        out_shape.append(jax.ShapeDtypeStruct((t, nrot), BF16))
    return pl.pallas_call(
        functools.partial(_att_inproj_kernel, rope=rope, nq=nq),
        grid=(t // tm,),
        in_specs=in_specs,
        out_specs=out_specs,
        out_shape=out_shape,
        compiler_params=_cparams(("arbitrary",)),
        name="att_inproj_rope" if rope else "att_inproj",
    )(*args)


def _attn_kernel(*refs, window, nb):
    if window:
        (sink_ref, qr_ref, qp_ref, kp_ref, kc_ref, kn_ref, vp_ref, vc_ref, vn_ref,
         kx_ref, vx_ref, o_ref) = refs
    else:
        sink_ref, qp_ref, kx_ref, vx_ref, o_ref = refs
    i = pl.program_id(1)
    rows = ATT_GROUP * BLOCK
    half_lane = lax.broadcasted_iota(jnp.int32, (BLOCK, LANES), 1) < HEAD_DIM
    ridx = lax.broadcasted_iota(jnp.int32, (rows, 1), 0)
    if window:
        qrow = lax.broadcasted_iota(jnp.int32, (rows, 3 * BLOCK), 0) & (BLOCK - 1)
        col = lax.broadcasted_iota(jnp.int32, (rows, 3 * BLOCK), 1)
        first_col = jnp.where(i > 0, 0, BLOCK)
        last_col = jnp.where(i < nb - 1, 3 * BLOCK - 1, 2 * BLOCK - 1)
        valid = (col >= jnp.maximum(qrow, first_col)) & (col <= jnp.minimum(qrow + 2 * WINDOW, last_col))

    def stack_heads(q_ref, g):
        parts = []
        for p in range(ATT_GROUP // 2):
            c0 = (g * (ATT_GROUP // 2) + p) * LANES
            pair = q_ref[0, :, c0:c0 + LANES]
            zero = jnp.zeros_like(pair)
            parts.append(jnp.where(half_lane, pair, zero))
            parts.append(jnp.where(half_lane, zero, pair))
        return jnp.concatenate(parts, axis=0)

    for g in range(ATT_KV_HEADS):
        gl = slice(g * LANES, (g + 1) * LANES)
        sink_col = jnp.full((rows, 1), sink_ref[g * ATT_GROUP + ATT_GROUP - 1], F32)
        for r in range(ATT_GROUP - 2, -1, -1):
            sink_col = jnp.where(ridx < (r + 1) * BLOCK, sink_ref[g * ATT_GROUP + r], sink_col)
        qp = stack_heads(qp_ref, g)
        s_ctx = _dot_nt(qp, kx_ref[0, :, gl])
        m = jnp.maximum(jnp.max(s_ctx, axis=-1, keepdims=True), sink_col)
        if window:
            qr = stack_heads(qr_ref, g)
            kw = jnp.concatenate([kp_ref[0, :, gl], kc_ref[0, :, gl], kn_ref[0, :, gl]], axis=0)
            s_win = jnp.where(valid, _dot_nt(qr, kw), NEG_INF)
            m = jnp.maximum(m, jnp.max(s_win, axis=-1, keepdims=True))
        p_ctx = jnp.exp(s_ctx - m)
        denom = jnp.sum(p_ctx, axis=-1, keepdims=True) + jnp.exp(sink_col - m)
        o = _dot(p_ctx.astype(BF16), vx_ref[0, :, gl])
        if window:
            p_win = jnp.exp(s_win - m)
            denom = denom + jnp.sum(p_win, axis=-1, keepdims=True)
            vw = jnp.concatenate([vp_ref[0, :, gl], vc_ref[0, :, gl], vn_ref[0, :, gl]], axis=0)
            o = o + _dot(p_win.astype(BF16), vw)
        o = o * (1.0 / denom)
        for p in range(ATT_GROUP // 2):
            a = o[(2 * p) * BLOCK:(2 * p + 1) * BLOCK]
            b = o[(2 * p + 1) * BLOCK:(2 * p + 2) * BLOCK]
            c0 = (g * (ATT_GROUP // 2) + p) * LANES
            o_ref[0, :, c0:c0 + LANES] = jnp.where(half_lane, a, b).astype(BF16)


def _attn_call(sink, qkv, rot, qkv_ctx, *, bsz, seq, n_ctx, window):
    nq = ATT_HEADS * HEAD_DIM
    nkv = ATT_KV_HEADS * LANES
    nb = seq // BLOCK
    qkv3 = qkv.reshape(bsz, seq, qkv.shape[1])
    ctx3 = qkv_ctx.reshape(bsz, n_ctx, qkv_ctx.shape[1])
    kcol = nq // nkv
    vcol = kcol + 1
    smem = pl.BlockSpec(memory_space=pltpu.SMEM)
    qspec = pl.BlockSpec((1, BLOCK, nq), lambda b, i: (b, i, 0))
    kx = pl.BlockSpec((1, n_ctx, nkv), lambda b, i: (b, 0, kcol))
    vx = pl.BlockSpec((1, n_ctx, nkv), lambda b, i: (b, 0, vcol))
    if window:
        rot3 = rot.reshape(bsz, seq, rot.shape[1])

        def kv_specs(colblk):
            return [
                pl.BlockSpec((1, BLOCK, nkv), lambda b, i: (b, jnp.maximum(i - 1, 0), colblk)),
                pl.BlockSpec((1, BLOCK, nkv), lambda b, i: (b, i, colblk)),
                pl.BlockSpec((1, BLOCK, nkv), lambda b, i: (b, jnp.minimum(i + 1, nb - 1), colblk)),
            ]
        in_specs = [smem, qspec, qspec] + kv_specs(kcol) + kv_specs(vcol) + [kx, vx]
        args = [sink, rot3, qkv3, rot3, rot3, rot3, qkv3, qkv3, qkv3, ctx3, ctx3]
    else:
        in_specs = [smem, qspec, kx, vx]
        args = [sink, qkv3, ctx3, ctx3]
    out = pl.pallas_call(
        functools.partial(_attn_kernel, window=window, nb=nb),
        grid=(bsz, nb),
        in_specs=in_specs,
        out_specs=pl.BlockSpec((1, BLOCK, nq), lambda b, i: (b, i, 0)),
        out_shape=jax.ShapeDtypeStruct((bsz, seq, nq), BF16),
        compiler_params=_cparams(("arbitrary", "arbitrary")),
        name="attn_window" if window else "attn_ctx",
    )(*args)
    return out.reshape(bsz * seq, nq)


def _outproj_ln_kernel(a_ref, w_ref, h_ref, gate_ref, g_ref, b_ref, o_ref, *, alpha):
    y = _dot(a_ref[...], w_ref[...])
    v = alpha * h_ref[...] + gate_ref[0] * y
    o_ref[...] = _layer_norm_rows(v, g_ref[...], b_ref[...])


def _outproj_ln_call(a, w, h, modt, ln_g, ln_b, *, alpha, seq, ctx_row, tm):
    t, k = a.shape
    d = w.shape[1]
    return pl.pallas_call(
        functools.partial(_outproj_ln_kernel, alpha=alpha),
        grid=(t // tm,),
        in_specs=[
            pl.BlockSpec((tm, k), lambda i: (i, 0)),
            pl.BlockSpec((k, d), lambda i: (0, 0)),
            pl.BlockSpec((tm, d), lambda i: (i, 0)),
            pl.BlockSpec((1, 1, d), _mod_index(2, tm, seq, ctx_row)),
            pl.BlockSpec((1, d), lambda i: (0, 0)),
            pl.BlockSpec((1, d), lambda i: (0, 0)),
        ],
        out_specs=pl.BlockSpec((tm, d), lambda i: (i, 0)),
        out_shape=jax.ShapeDtypeStruct((t, d), F32),
        compiler_params=_cparams(("arbitrary",)),
        name="outproj_ln",
    )(a, w, h, modt, ln_g.reshape(1, d), ln_b.reshape(1, d))


def _mlp_ln_kernel(h_ref, sh_ref, sc_ref, gate_ref, w1_ref, w2_ref, g_ref, b_ref, o_ref, u_scr, acc_scr, *, alpha):
    j = pl.program_id(1)

    @pl.when(j == 0)
    def _():
        u_scr[...] = (h_ref[...] * (1.0 + sc_ref[0]) + sh_ref[0]).astype(BF16)
        acc_scr[...] = jnp.zeros_like(acc_scr)

    a = jnp.maximum(_dot(u_scr[...], w1_ref[...]), 0.0)
    acc_scr[...] += _dot((a * a).astype(BF16), w2_ref[...])

    @pl.when(j == pl.num_programs(1) - 1)
    def _():
        v = alpha * h_ref[...] + gate_ref[0] * acc_scr[...]
        o_ref[...] = _layer_norm_rows(v, g_ref[...], b_ref[...])


def _mlp_ln_call(h, modt, w1, w2, ln_g, ln_b, *, alpha, seq, ctx_row, tm, tf):
    t, d = h.shape
    ff = w1.shape[1]
    return pl.pallas_call(
        functools.partial(_mlp_ln_kernel, alpha=alpha),
        grid=(t // tm, ff // tf),
        in_specs=[
            pl.BlockSpec((tm, d), lambda i, j: (i, 0)),
            pl.BlockSpec((1, 1, d), _mod_index(3, tm, seq, ctx_row)),
            pl.BlockSpec((1, 1, d), _mod_index(4, tm, seq, ctx_row)),
            pl.BlockSpec((1, 1, d), _mod_index(5, tm, seq, ctx_row)),
            pl.BlockSpec((d, tf), lambda i, j: (0, j)),
            pl.BlockSpec((tf, d), lambda i, j: (j, 0)),
            pl.BlockSpec((1, d), lambda i, j: (0, 0)),
            pl.BlockSpec((1, d), lambda i, j: (0, 0)),
        ],
        out_specs=pl.BlockSpec((tm, d), lambda i, j: (i, 0)),
        out_shape=jax.ShapeDtypeStruct((t, d), F32),
        scratch_shapes=[pltpu.VMEM((tm, d), BF16), pltpu.VMEM((tm, d), F32)],
        compiler_params=_cparams(("arbitrary", "arbitrary")),
        name="mlp_ln",
    )(h, modt, modt, modt, w1, w2, ln_g.reshape(1, d), ln_b.reshape(1, d))


def _ssm_inproj_kernel(h_ref, sh_ref, sc_ref, w_ref, wdt_ref, dtb_ref, o_ref, dt_ref, u_scr):
    j = pl.program_id(1)

    @pl.when(j == 0)
    def _():
        u = (h_ref[...] * (1.0 + sc_ref[0]) + sh_ref[0]).astype(BF16)
        u_scr[...] = u
        raw = _dot(u, wdt_ref[...]) + dtb_ref[...]
        sp = jnp.maximum(raw, 0.0) + jnp.log1p(jnp.exp(-jnp.abs(raw)))
        lane = lax.broadcasted_iota(jnp.int32, sp.shape, 1)
        per_group = LANES // 2 // SSM_GROUPS
        for g in range(SSM_GROUPS):
            shifted = sp if g == 0 else pltpu.roll(sp, LANES - per_group * g, 1)
            dt_ref[g] = jnp.where(lane < per_group, shifted, 0.0)

    o_ref[0] = _dot(u_scr[...], w_ref[...]).astype(BF16)


def _ssm_inproj_call(h, modt, w_main, w_dt, dt_bias, *, seq, ctx_row, tm, tn):
    t, d = h.shape
    ntile = w_main.shape[1] // tn
    return pl.pallas_call(
        _ssm_inproj_kernel,
        grid=(t // tm, ntile),
        in_specs=[
            pl.BlockSpec((tm, d), lambda i, j: (i, 0)),
            pl.BlockSpec((1, 1, d), _mod_index(0, tm, seq, ctx_row)),
            pl.BlockSpec((1, 1, d), _mod_index(1, tm, seq, ctx_row)),
            pl.BlockSpec((d, tn), lambda i, j: (0, j)),
            pl.BlockSpec((d, LANES), lambda i, j: (0, 0)),
            pl.BlockSpec((1, LANES), lambda i, j: (0, 0)),
        ],
        out_specs=[
            pl.BlockSpec((1, tm, tn), lambda i, j: (j, i, 0)),
            pl.BlockSpec((SSM_GROUPS, tm, LANES), lambda i, j: (0, i, 0)),
        ],
        out_shape=[
            jax.ShapeDtypeStruct((ntile, t, tn), BF16),
            jax.ShapeDtypeStruct((SSM_GROUPS, t, LANES), F32),
        ],
        scratch_shapes=[pltpu.VMEM((tm, d), BF16)],
        compiler_params=_cparams(("arbitrary", "arbitrary")),
        name="ssm_inproj",
    )(h, modt, modt, w_main, w_dt, dt_bias)


def _conv_kernel(prev_ref, cur_ref, next_ref, w_ref, b_ref, o_ref, *, nt):
    i = pl.program_id(2)
    tl = cur_ref.shape[1]
    halo = prev_ref.shape[1]
    prev = jnp.where(i > 0, prev_ref[0].astype(F32), 0.0)
    nxt = jnp.where(i < nt - 1, next_ref[0].astype(F32), 0.0)
    xx = jnp.concatenate([prev, cur_ref[0].astype(F32), nxt], axis=0)
    total = tl + 2 * halo
    w = w_ref[0]
    acc = jnp.zeros((tl, xx.shape[1]), F32) + b_ref[0]
    for k in range(SSM_CONV_W):
        shift = SSM_CONV_W // 2 - k
        rolled = xx if shift == 0 else pltpu.roll(xx, shift % total, 0)
        acc = acc + w[k:k + 1] * rolled[halo:halo + tl]
    o_ref[0] = (acc * _sigmoid(acc)).astype(BF16)


def _conv_call(zxbc, conv_w, conv_b, *, bsz, seq, first_tile, tl):
    ntile_all, t, tn = zxbc.shape
    ntile = ntile_all - first_tile
    nt = seq // tl
    halo = BF16_SUBLANES
    hb = tl // halo
    last_h = t // halo - 1
    return pl.pallas_call(
        functools.partial(_conv_kernel, nt=nt),
        grid=(ntile, bsz, nt),
        in_specs=[
            pl.BlockSpec((1, halo, tn), lambda c, b, i: (first_tile + c, jnp.maximum((b * nt + i) * hb - 1, 0), 0)),
            pl.BlockSpec((1, tl, tn), lambda c, b, i: (first_tile + c, b * nt + i, 0)),
            pl.BlockSpec((1, halo, tn), lambda c, b, i: (first_tile + c, jnp.minimum((b * nt + i + 1) * hb, last_h), 0)),
            pl.BlockSpec((1, 8, tn), lambda c, b, i: (c, 0, 0)),
            pl.BlockSpec((1, 1, tn), lambda c, b, i: (c, 0, 0)),
        ],
        out_specs=pl.BlockSpec((1, tl, tn), lambda c, b, i: (c, b * nt + i, 0)),
        out_shape=jax.ShapeDtypeStruct((ntile, t, tn), BF16),
        compiler_params=_cparams(("arbitrary", "arbitrary", "arbitrary")),
        name="ssm_conv",
    )(zxbc, zxbc, zxbc, conv_w, conv_b)


HEADS_PER_GROUP = 8


def _expand_heads(v, off, nrows):
    lane = lax.broadcasted_iota(jnp.int32, (nrows, LANES), 1)
    parts = []
    for p in range(HEADS_PER_GROUP // 2):
        a = jnp.broadcast_to(v[:, off + 2 * p:off + 2 * p + 1], (nrows, LANES))
        b = jnp.broadcast_to(v[:, off + 2 * p + 1:off + 2 * p + 2], (nrows, LANES))
        parts.append(jnp.where(lane < SSM_HEAD_DIM, a, b))
    return jnp.concatenate(parts, axis=1)


def _cumsum_rows(a):
    q = a.shape[0]
    tril = (lax.broadcasted_iota(jnp.int32, (q, q), 0) >= lax.broadcasted_iota(jnp.int32, (q, q), 1))
    tril = jnp.where(tril, 1.0, 0.0).astype(BF16)
    hi = a.astype(BF16)
    r1 = a - hi.astype(F32)
    mid = r1.astype(BF16)
    lo = (r1 - mid.astype(F32)).astype(BF16)
    return _dot(tril, hi) + _dot(tril, mid) + _dot(tril, lo)


def _chunk_terms(dt, a_neg):
    q = dt.shape[0]
    lane = lax.broadcasted_iota(jnp.int32, (q, LANES), 1)
    is_bwd = lane >= HEADS_PER_GROUP
    a = dt * a_neg
    acs = _cumsum_rows(a)
    tot = acs[q - 1:q, :]
    e = jnp.where(is_bwd, acs - a, acs)
    w_state = dt * jnp.exp(jnp.where(is_bwd, e, tot - acs))
    f_out = jnp.exp(jnp.where(is_bwd, tot - e, acs))
    dec = jnp.exp(tot)
    return e, w_state, f_out, dec


def _state_update(s_scr, x, bm, w_state, dec, off):
    q = x.shape[0]
    xw = (x.astype(F32) * _expand_heads(w_state, off, q)).astype(BF16)
    s_scr[...] = s_scr[...] * _expand_heads(dec, off, 1) + _dot_tn(bm, xw)


def _ssd_kernel(alog_ref, d_ref, ng_ref, xc_ref, bc_ref, dtc_ref, xl_ref, bl_ref, cl_ref, zl_ref, dtl_ref,
                y_ref, sf_scr, sb_scr, sbs_scr, *, ncc, ncl):
    s = pl.program_id(2)
    a_neg = -jnp.exp(alog_ref[0])
    q = SSM_CHUNK
    bwd = HEADS_PER_GROUP

    @pl.when(s == 0)
    def _():
        sf_scr[...] = jnp.zeros_like(sf_scr)
        sb_scr[...] = jnp.zeros_like(sb_scr)

    @pl.when(s < ncc)
    def _():
        _, w_state, _, dec = _chunk_terms(dtc_ref[0], a_neg)
        _state_update(sb_scr, xc_ref[0], bc_ref[0], w_state, dec, bwd)

    @pl.when((s >= ncc) & (s < 2 * ncc))
    def _():
        _, w_state, _, dec = _chunk_terms(dtc_ref[0], a_neg)
        _state_update(sf_scr, xc_ref[0], bc_ref[0], w_state, dec, 0)

    @pl.when((s >= 2 * ncc) & (s < 2 * ncc + ncl))
    def _():
        lc = ncl - 1 - (s - 2 * ncc)
        sbs_scr[lc] = sb_scr[...].astype(BF16)
        _, w_state, _, dec = _chunk_terms(dtl_ref[0], a_neg)
        _state_update(sb_scr, xl_ref[0], bl_ref[0], w_state, dec, bwd)

    @pl.when(s >= 2 * ncc + ncl)
    def _():
        lc = s - 2 * ncc - ncl
        dt = dtl_ref[0]
        e, w_state, f_out, dec = _chunk_terms(dt, a_neg)
        e_t = e.T
        dt_t = dt.T
        x = xl_ref[0]
        bm = bl_ref[0]
        cm = cl_ref[0]
        cb = _dot_nt(cm, bm)
        li = lax.broadcasted_iota(jnp.int32, (q, q), 0)
        si = lax.broadcasted_iota(jnp.int32, (q, q), 1)
        lower = li >= si
        upper = li <= si
        lane = lax.broadcasted_iota(jnp.int32, (q, LANES), 1)
        first = lane < SSM_HEAD_DIM
        y_f = _dot(cm, sf_scr[...].astype(BF16))
        y_b = _dot(cm, sbs_scr[lc])
        y_parts = []
        for p in range(HEADS_PER_GROUP // 2):
            ms = []
            for r in (2 * p, 2 * p + 1):
                col_f = jnp.broadcast_to(e[:, r:r + 1], (q, q))
                col_b = jnp.broadcast_to(e[:, bwd + r:bwd + r + 1], (q, q))
                arg = jnp.where(lower, col_f - e_t[r:r + 1, :], e_t[bwd + r:bwd + r + 1, :] - col_b)
                dsel = jnp.where(lower, dt_t[r:r + 1, :], 0.0) + jnp.where(upper, dt_t[bwd + r:bwd + r + 1, :], 0.0)
                ms.append((cb * jnp.exp(arg) * dsel).astype(BF16))
            xp = x[:, p * LANES:(p + 1) * LANES]
            zero = jnp.zeros_like(xp)
            x_bd = jnp.concatenate([jnp.where(first, xp, zero), jnp.where(first, zero, xp)], axis=0)
            y_parts.append(_dot(jnp.concatenate(ms, axis=1), x_bd))
        y = (jnp.concatenate(y_parts, axis=1)
             + y_f * _expand_heads(f_out, 0, q) + y_b * _expand_heads(f_out, bwd, q)
             + x.astype(F32) * d_ref[0])
        z = zl_ref[0].astype(F32)
        yy = y * (z * _sigmoid(z))
        ms_ = jnp.mean(yy * yy, axis=-1, keepdims=True)
        y_ref[0] = (yy * lax.rsqrt(ms_ + RMS_EPS) * ng_ref[0]).astype(BF16)
        _state_update(sf_scr, x, bm, w_state, dec, 0)


def _ssd_call(a_log, d_row, norm_g, zx_c, xbc_c, dt_c, zx_l, xbc_l, dt_l, *, bsz, n_ctx, seq):
    q = SSM_CHUNK
    ncc = n_ctx // q
    ncl = seq // q
    gcols = zx_l.shape[2]
    nsteps = 2 * ncc + 2 * ncl
    n_state = SSM_STATE
    xt = SSM_GROUPS

    def cchunk(s):
        return jnp.where(s < ncc, ncc - 1 - s, jnp.where(s < 2 * ncc, s - ncc, ncc - 1))

    def lchunk(s):
        t = s - 2 * ncc
        return jnp.where(t < 0, ncl - 1, jnp.where(t < ncl, ncl - 1 - t, t - ncl))

    def lchunk_fwd(s):
        return jnp.maximum(s - 2 * ncc - ncl, 0)

    in_specs = [
        pl.BlockSpec((1, 1, LANES), lambda b, g, s: (g, 0, 0)),
        pl.BlockSpec((1, 1, gcols), lambda b, g, s: (g, 0, 0)),
        pl.BlockSpec((1, 1, gcols), lambda b, g, s: (g, 0, 0)),
        pl.BlockSpec((1, q, gcols), lambda b, g, s: (g, b * ncc + cchunk(s), 0)),
        pl.BlockSpec((1, q, n_state), lambda b, g, s: (xt, b * ncc + cchunk(s), g)),
        pl.BlockSpec((1, q, LANES), lambda b, g, s: (g, b * ncc + cchunk(s), 0)),
        pl.BlockSpec((1, q, gcols), lambda b, g, s: (g, b * ncl + lchunk(s), 0)),
        pl.BlockSpec((1, q, n_state), lambda b, g, s: (xt, b * ncl + lchunk(s), g)),
        pl.BlockSpec((1, q, n_state), lambda b, g, s: (xt + 1, b * ncl + lchunk_fwd(s), g)),
        pl.BlockSpec((1, q, gcols), lambda b, g, s: (g, b * ncl + lchunk_fwd(s), 0)),
        pl.BlockSpec((1, q, LANES), lambda b, g, s: (g, b * ncl + lchunk(s), 0)),
    ]
    return pl.pallas_call(
        functools.partial(_ssd_kernel, ncc=ncc, ncl=ncl),
        grid=(bsz, SSM_GROUPS, nsteps),
        in_specs=in_specs,
        out_specs=pl.BlockSpec((1, q, gcols), lambda b, g, s: (0, b * ncl + lchunk_fwd(s), g)),
        out_shape=jax.ShapeDtypeStruct((1, bsz * seq, SSM_GROUPS * gcols), BF16),
        scratch_shapes=[
            pltpu.VMEM((n_state, gcols), F32),
            pltpu.VMEM((n_state, gcols), F32),
            pltpu.VMEM((ncl, n_state, gcols), BF16),
        ],
        compiler_params=_cparams(("arbitrary", "arbitrary", "arbitrary")),
        name="ssd_scan",
    )(a_log, d_row, norm_g, xbc_c, xbc_c, dt_c, xbc_l, xbc_l, xbc_l, zx_l, dt_l)[0]


def _rope_tables(seq):
    rows = seq // GRID_W
    row = jnp.repeat(jnp.arange(rows), GRID_W).astype(F32)
    col = jnp.tile(jnp.arange(GRID_W), rows).astype(F32)
    inv_freq = ROPE_BASE ** (-jnp.arange(0, ROPE_AXIS_DIM, 2, dtype=F32) / ROPE_AXIS_DIM)
    ang_r = row[:, None] * inv_freq[None, :]
    ang_c = col[:, None] * inv_freq[None, :]
    cos = jnp.concatenate([jnp.cos(ang_r)] * 2 + [jnp.cos(ang_c)] * 2, axis=-1)
    sin = jnp.concatenate([-jnp.sin(ang_r), jnp.sin(ang_r), -jnp.sin(ang_c), jnp.sin(ang_c)], axis=-1)
    reps = LANES // HEAD_DIM
    return jnp.tile(cos, (1, reps)), jnp.tile(sin, (1, reps))


def _dup_heads(w):
    d, n = w.shape
    w = w.reshape(d, n // HEAD_DIM, 1, HEAD_DIM)
    return jnp.broadcast_to(w, (d, n // HEAD_DIM, LANES // HEAD_DIM, HEAD_DIM)).reshape(d, -1)


def _attention_layer(h_lat, h_ctx, modt, w_in, w_out, sink, ln_g, ln_b, *, bsz, seq, n_ctx, ctx_row, alpha, tm):
    nq = ATT_HEADS * HEAD_DIM
    nk = ATT_KV_HEADS * HEAD_DIM
    wq, wk, wv = w_in[:, :nq], w_in[:, nq:nq + nk], w_in[:, nq + nk:]
    wk, wv = _dup_heads(wk), _dup_heads(wv)
    w_cat = jnp.concatenate([wq, wk, wv], axis=1).astype(BF16)
    nrot = nq + wk.shape[1]
    tables = _rope_tables(seq)
    qkv_l, rot_l = _att_inproj_call(h_lat, modt, w_cat, tables, seq=seq, ctx_row=None, nq=nq, nrot=nrot, tm=tm)
    (qkv_c,) = _att_inproj_call(h_ctx, modt, w_cat, None, seq=n_ctx, ctx_row=ctx_row, nq=nq, nrot=nrot,
                                tm=min(tm, h_ctx.shape[0]))
    o_l = _attn_call(sink, qkv_l, rot_l, qkv_c, bsz=bsz, seq=seq, n_ctx=n_ctx, window=True)
    o_c = _attn_call(sink, qkv_c, None, qkv_c, bsz=bsz, seq=n_ctx, n_ctx=n_ctx, window=False)
    w_o = w_out.astype(BF16)
    h_lat = _outproj_ln_call(o_l, w_o, h_lat, modt, ln_g, ln_b, alpha=alpha, seq=seq, ctx_row=None, tm=tm)
    h_ctx = _outproj_ln_call(o_c, w_o, h_ctx, modt, ln_g, ln_b, alpha=alpha, seq=n_ctx, ctx_row=ctx_row,
                             tm=min(tm, h_ctx.shape[0]))
    return h_lat, h_ctx


def _ssm_layer(h_lat, h_ctx, modt, w_in, conv_w, conv_b, dt_bias, a_log, d_skip, norm_g, w_out, ln_g, ln_b,
               *, bsz, seq, n_ctx, ctx_row, alpha, tm):
    d_inner = w_out.shape[0]
    heads = d_inner // SSM_HEAD_DIM
    gcols = d_inner // SSM_GROUPS
    hpg = heads // SSM_GROUPS
    conv_dim = d_inner + 2 * SSM_GROUPS * SSM_STATE
    n_main = d_inner + conv_dim
    assert hpg == HEADS_PER_GROUP and gcols == SSM_GROUPS * SSM_STATE
    w_main = w_in[:, :n_main].astype(BF16)

    def regroup(v):
        lead = v.shape[:-2]
        v = v.reshape(lead + (2, SSM_GROUPS, hpg))
        v = jnp.moveaxis(v, -3, -2).reshape(lead + (SSM_GROUPS, 2 * hpg))
        return v

    w_dt = regroup(w_in[:, n_main:].reshape(-1, 2, heads)).reshape(-1, 2 * heads)
    w_dt = jnp.pad(w_dt, ((0, 0), (0, LANES - 2 * heads))).astype(BF16)
    dtb = jnp.pad(regroup(dt_bias).reshape(1, 2 * heads), ((0, 0), (0, LANES - 2 * heads)))
    a_log_g = jnp.pad(regroup(a_log), ((0, 0), (0, LANES - 2 * hpg))).reshape(SSM_GROUPS, 1, LANES)
    d_row = jnp.repeat(d_skip, SSM_HEAD_DIM).reshape(SSM_GROUPS, 1, gcols)
    ng = norm_g.reshape(SSM_GROUPS, 1, gcols)
    ntile_conv = conv_dim // gcols
    cw = jnp.pad(conv_w, ((0, 8 - SSM_CONV_W), (0, 0))).reshape(8, ntile_conv, gcols).transpose(1, 0, 2)
    cb = conv_b.reshape(ntile_conv, 1, gcols)

    zx_l, dt_l = _ssm_inproj_call(h_lat, modt, w_main, w_dt, dtb, seq=seq, ctx_row=None, tm=tm, tn=gcols)
    zx_c, dt_c = _ssm_inproj_call(h_ctx, modt, w_main, w_dt, dtb, seq=n_ctx, ctx_row=ctx_row,
                                  tm=min(tm, h_ctx.shape[0]), tn=gcols)
    first = d_inner // gcols
    xbc_l = _conv_call(zx_l, cw, cb, bsz=bsz, seq=seq, first_tile=first, tl=min(512, seq))
    xbc_c = _conv_call(zx_c, cw, cb, bsz=bsz, seq=n_ctx, first_tile=first, tl=min(512, n_ctx))
    y = _ssd_call(a_log_g, d_row, ng, zx_c, xbc_c, dt_c, zx_l, xbc_l, dt_l, bsz=bsz, n_ctx=n_ctx, seq=seq)
    return _outproj_ln_call(y, w_out.astype(BF16), h_lat, modt, ln_g, ln_b, alpha=alpha, seq=seq, ctx_row=None, tm=tm)


def kernel(x, c, ctx, c_ctx, w_mod, b_mod, ln_mix_g, ln_mix_b, ln_ff_g, ln_ff_b, att_w_in, att_w_out, att_sink,
           ssm_w_in, ssm_conv_w, ssm_conv_b, ssm_dt_bias, ssm_a_log, ssm_d, ssm_norm_g, ssm_w_out, ff_w1, ff_w2):
    bsz, seq, d = x.shape
    n_ctx = ctx.shape[1]
    depth = w_mod.shape[0]
    alpha = (2.0 * depth) ** 0.25
    tm = 512
    ctx_row = bsz
    rows = -(-(bsz + 1) // MOD_ROWS_ALIGN) * MOD_ROWS_ALIGN
    cc = jnp.concatenate([c, c_ctx[None, :], jnp.zeros((rows - bsz - 1, d), F32)], axis=0)
    mod = _mod_call(cc, w_mod, b_mod)
    h_lat = x.reshape(bsz * seq, d)
    h_ctx = ctx.reshape(bsz * n_ctx, d)
    for i in range(depth):
        last = i == depth - 1
        j = i // N_MIXERS
        modt = mod[i].reshape(rows * N_MOD, 1, d)
        if i % N_MIXERS == 0:
            h_lat, h_ctx_mix = _attention_layer(h_lat, h_ctx, modt, att_w_in[j], att_w_out[j], att_sink[j],
                                                ln_mix_g[i], ln_mix_b[i], bsz=bsz, seq=seq, n_ctx=n_ctx,
                                                ctx_row=ctx_row, alpha=alpha, tm=tm)
        else:
            if not last:
                raise NotImplementedError("context outputs of an SSD layer are only needed when it is not last")
            h_lat = _ssm_layer(h_lat, h_ctx, modt, ssm_w_in[j], ssm_conv_w[j], ssm_conv_b[j], ssm_dt_bias[j],
                               ssm_a_log[j], ssm_d[j], ssm_norm_g[j], ssm_w_out[j], ln_mix_g[i], ln_mix_b[i],
                               bsz=bsz, seq=seq, n_ctx=n_ctx, ctx_row=ctx_row, alpha=alpha, tm=tm)
            h_ctx_mix = None
        w1 = ff_w1[i].astype(BF16)
        w2 = ff_w2[i].astype(BF16)
        h_lat = _mlp_ln_call(h_lat, modt, w1, w2, ln_ff_g[i], ln_ff_b[i], alpha=alpha, seq=seq, ctx_row=None,
                             tm=tm, tf=1024)
        if not last:
            h_ctx = _mlp_ln_call(h_ctx_mix, modt, w1, w2, ln_ff_g[i], ln_ff_b[i], alpha=alpha, seq=n_ctx,
                                 ctx_row=ctx_row, tm=min(tm, h_ctx_mix.shape[0]), tf=1024)
    return h_lat.reshape(bsz, seq, d)
```

```python
import functools

import jax
import jax.numpy as jnp
from jax import lax
from jax.experimental import pallas as pl
from jax.experimental.pallas import tpu as pltpu

F32 = jnp.float32
BF16 = jnp.bfloat16

GRID_W = 64
N_MIXERS = 2
ATT_HEADS = 16
ATT_KV_HEADS = 4
HEAD_DIM = 64
ATT_GROUP = ATT_HEADS // ATT_KV_HEADS
WINDOW = 128
BLOCK = 128
ROPE_BASE = 10000.0
ROPE_AXIS_DIM = HEAD_DIM // 2

SSM_HEAD_DIM = 64
SSM_GROUPS = 4
SSM_STATE = 128
SSM_CONV_W = 5
SSM_CHUNK = 128

N_MOD = 6
LN_EPS = 1e-5
RMS_EPS = 1e-5
NEG_INF = -1e30

LANES = 128
BF16_SUBLANES = 16
MOD_ROWS_ALIGN = 16
VMEM_LIMIT = 48 * 1024 * 1024


def _cparams(sem):
    return pltpu.CompilerParams(dimension_semantics=sem, vmem_limit_bytes=VMEM_LIMIT)


def _sigmoid(x):
    return 1.0 / (1.0 + jnp.exp(-x))


def _dot(a, b):
    return jnp.dot(a, b, preferred_element_type=F32)


def _dot_nt(a, b):
    return lax.dot_general(a, b, (((1,), (1,)), ((), ())), preferred_element_type=F32)


def _dot_tn(a, b):
    return lax.dot_general(a, b, (((0,), (0,)), ((), ())), preferred_element_type=F32)


def _layer_norm_rows(v, g, b):
    mu = jnp.mean(v, axis=-1, keepdims=True)
    d = v - mu
    var = jnp.mean(d * d, axis=-1, keepdims=True)
    return d * lax.rsqrt(var + LN_EPS) * g + b


def _mod_kernel(c_ref, w_ref, b_ref, o_ref):
    c = c_ref[...]
    s = (c * _sigmoid(c)).astype(BF16)
    o_ref[0] = _dot(s, w_ref[0].astype(BF16)) + b_ref[0]


def _mod_call(cc, w_mod, b_mod):
    depth, d, n = w_mod.shape
    rows = cc.shape[0]
    tn = 1024
    return pl.pallas_call(
        _mod_kernel,
        grid=(depth, n // tn),
        in_specs=[
            pl.BlockSpec((rows, d), lambda l, j: (0, 0)),
            pl.BlockSpec((1, d, tn), lambda l, j: (l, 0, j)),
            pl.BlockSpec((1, 1, tn), lambda l, j: (l, 0, j)),
        ],
        out_specs=pl.BlockSpec((1, rows, tn), lambda l, j: (l, 0, j)),
        out_shape=jax.ShapeDtypeStruct((depth, rows, n), F32),
        compiler_params=_cparams(("arbitrary", "arbitrary")),
        name="adaln_mod",
    )(cc, w_mod, b_mod.reshape(depth, 1, n))


def _mod_index(which, tm, seq, ctx_row):
    if ctx_row is None:
        return lambda i, *_: ((i * tm // seq) * N_MOD + which, 0, 0)
    return lambda i, *_: (ctx_row * N_MOD + which, 0, 0)


def _att_inproj_kernel(*refs, rope, nq):
    if rope:
        x_ref, sh_ref, sc_ref, w_ref, cos_ref, sin_ref, qkv_ref, rot_ref = refs
    else:
        x_ref, sh_ref, sc_ref, w_ref, qkv_ref = refs
    u = (x_ref[...] * (1.0 + sc_ref[0]) + sh_ref[0]).astype(BF16)
    acc = _dot(u, w_ref[...])
    n = acc.shape[1]
    nrot = rot_ref.shape[1] if rope else 0
    if rope:
        cos = cos_ref[...]
        sin = sin_ref[...]
        lane = lax.broadcasted_iota(jnp.int32, cos.shape, 1)
        first_half = (lane & (ROPE_AXIS_DIM - 1)) < (ROPE_AXIS_DIM // 2)
    for cidx in range(n // LANES):
        chunk = acc[:, cidx * LANES:(cidx + 1) * LANES]
        if cidx * LANES < nq:
            chunk = chunk * (HEAD_DIM ** -0.5)
        qkv_ref[:, cidx * LANES:(cidx + 1) * LANES] = chunk.astype(BF16)
        if rope and cidx * LANES < nrot:
            half = ROPE_AXIS_DIM // 2
            swapped = jnp.where(first_half, pltpu.roll(chunk, LANES - half, 1), pltpu.roll(chunk, half, 1))
            rot_ref[:, cidx * LANES:(cidx + 1) * LANES] = (chunk * cos + swapped * sin).astype(BF16)


def _att_inproj_call(h, modt, w, tables, *, seq, ctx_row, nq, nrot, tm):
    t, d = h.shape
    n = w.shape[1]
    rope = tables is not None
    in_specs = [
        pl.BlockSpec((tm, d), lambda i: (i, 0)),
        pl.BlockSpec((1, 1, d), _mod_index(0, tm, seq, ctx_row)),
        pl.BlockSpec((1, 1, d), _mod_index(1, tm, seq, ctx_row)),
        pl.BlockSpec((d, n), lambda i: (0, 0)),
    ]
    args = [h, modt, modt, w]
    out_specs = [pl.BlockSpec((tm, n), lambda i: (i, 0))]
    out_shape = [jax.ShapeDtypeStruct((t, n), BF16)]
    if rope:
        per_seq = seq // tm
        in_specs += [pl.BlockSpec((tm, LANES), lambda i: (i % per_seq, 0))] * 2
        args += list(tables)
        out_specs.append(pl.BlockSpec((tm, nrot), lambda i: (i, 0)))
        out_shape.append(jax.ShapeDtypeStruct((t, nrot), BF16))
    return pl.pallas_call(
        functools.partial(_att_inproj_kernel, rope=rope, nq=nq),
        grid=(t // tm,),
        in_specs=in_specs,
        out_specs=out_specs,
        out_shape=out_shape,
        compiler_params=_cparams(("arbitrary",)),
        name="att_inproj_rope" if rope else "att_inproj",
    )(*args)


def _attn_kernel(*refs, window, nb):
    if window:
        (sink_ref, qr_ref, qp_ref, kp_ref, kc_ref, kn_ref, vp_ref, vc_ref, vn_ref,
         kx_ref, vx_ref, o_ref) = refs
    else:
        sink_ref, qp_ref, kx_ref, vx_ref, o_ref = refs
    i = pl.program_id(1)
    rows = ATT_GROUP * BLOCK
    half_lane = lax.broadcasted_iota(jnp.int32, (BLOCK, LANES), 1) < HEAD_DIM
    ridx = lax.broadcasted_iota(jnp.int32, (rows, 1), 0)
    if window:
        qrow = lax.broadcasted_iota(jnp.int32, (rows, 3 * BLOCK), 0) & (BLOCK - 1)
        col = lax.broadcasted_iota(jnp.int32, (rows, 3 * BLOCK), 1)
        first_col = jnp.where(i > 0, 0, BLOCK)
        last_col = jnp.where(i < nb - 1, 3 * BLOCK - 1, 2 * BLOCK - 1)
        valid = (col >= jnp.maximum(qrow, first_col)) & (col <= jnp.minimum(qrow + 2 * WINDOW, last_col))

    def stack_heads(q_ref, g):
        parts = []
        for p in range(ATT_GROUP // 2):
            c0 = (g * (ATT_GROUP // 2) + p) * LANES
            pair = q_ref[0, :, c0:c0 + LANES]
            zero = jnp.zeros_like(pair)
            parts.append(jnp.where(half_lane, pair, zero))
            parts.append(jnp.where(half_lane, zero, pair))
        return jnp.concatenate(parts, axis=0)

    for g in range(ATT_KV_HEADS):
        gl = slice(g * LANES, (g + 1) * LANES)
        sink_col = jnp.full((rows, 1), sink_ref[g * ATT_GROUP + ATT_GROUP - 1], F32)
        for r in range(ATT_GROUP - 2, -1, -1):
            sink_col = jnp.where(ridx < (r + 1) * BLOCK, sink_ref[g * ATT_GROUP + r], sink_col)
        qp = stack_heads(qp_ref, g)
        s_ctx = _dot_nt(qp, kx_ref[0, :, gl])
        m = jnp.maximum(jnp.max(s_ctx, axis=-1, keepdims=True), sink_col)
        if window:
            qr = stack_heads(qr_ref, g)
            kw = jnp.concatenate([kp_ref[0, :, gl], kc_ref[0, :, gl], kn_ref[0, :, gl]], axis=0)
            s_win = jnp.where(valid, _dot_nt(qr, kw), NEG_INF)
            m = jnp.maximum(m, jnp.max(s_win, axis=-1, keepdims=True))
        p_ctx = jnp.exp(s_ctx - m)
        denom = jnp.sum(p_ctx, axis=-1, keepdims=True) + jnp.exp(sink_col - m)
        o = _dot(p_ctx.astype(BF16), vx_ref[0, :, gl])
        if window:
            p_win = jnp.exp(s_win - m)
            denom = denom + jnp.sum(p_win, axis=-1, keepdims=True)
            vw = jnp.concatenate([vp_ref[0, :, gl], vc_ref[0, :, gl], vn_ref[0, :, gl]], axis=0)
            o = o + _dot(p_win.astype(BF16), vw)
        o = o * (1.0 / denom)
        for p in range(ATT_GROUP // 2):
            a = o[(2 * p) * BLOCK:(2 * p + 1) * BLOCK]
            b = o[(2 * p + 1) * BLOCK:(2 * p + 2) * BLOCK]
            c0 = (g * (ATT_GROUP // 2) + p) * LANES
            o_ref[0, :, c0:c0 + LANES] = jnp.where(half_lane, a, b).astype(BF16)


def _attn_call(sink, qkv, rot, qkv_ctx, *, bsz, seq, n_ctx, window):
    nq = ATT_HEADS * HEAD_DIM
    nkv = ATT_KV_HEADS * LANES
    nb = seq // BLOCK
    qkv3 = qkv.reshape(bsz, seq, qkv.shape[1])
    ctx3 = qkv_ctx.reshape(bsz, n_ctx, qkv_ctx.shape[1])
    kcol = nq // nkv
    vcol = kcol + 1
    smem = pl.BlockSpec(memory_space=pltpu.SMEM)
    qspec = pl.BlockSpec((1, BLOCK, nq), lambda b, i: (b, i, 0))
    kx = pl.BlockSpec((1, n_ctx, nkv), lambda b, i: (b, 0, kcol))
    vx = pl.BlockSpec((1, n_ctx, nkv), lambda b, i: (b, 0, vcol))
    if window:
        rot3 = rot.reshape(bsz, seq, rot.shape[1])

        def kv_specs(colblk):
            return [
                pl.BlockSpec((1, BLOCK, nkv), lambda b, i: (b, jnp.maximum(i - 1, 0), colblk)),
                pl.BlockSpec((1, BLOCK, nkv), lambda b, i: (b, i, colblk)),
                pl.BlockSpec((1, BLOCK, nkv), lambda b, i: (b, jnp.minimum(i + 1, nb - 1), colblk)),
            ]
        in_specs = [smem, qspec, qspec] + kv_specs(kcol) + kv_specs(vcol) + [kx, vx]
        args = [sink, rot3, qkv3, rot3, rot3, rot3, qkv3, qkv3, qkv3, ctx3, ctx3]
    else:
        in_specs = [smem, qspec, kx, vx]
        args = [sink, qkv3, ctx3, ctx3]
    out = pl.pallas_call(
        functools.partial(_attn_kernel, window=window, nb=nb),
        grid=(bsz, nb),
        in_specs=in_specs,
        out_specs=pl.BlockSpec((1, BLOCK, nq), lambda b, i: (b, i, 0)),
        out_shape=jax.ShapeDtypeStruct((bsz, seq, nq), BF16),
        compiler_params=_cparams(("arbitrary", "arbitrary")),
        name="attn_window" if window else "attn_ctx",
    )(*args)
    return out.reshape(bsz * seq, nq)


def _outproj_ln_kernel(a_ref, w_ref, h_ref, gate_ref, g_ref, b_ref, o_ref, *, alpha):
    y = _dot(a_ref[...], w_ref[...])
    v = alpha * h_ref[...] + gate_ref[0] * y
    o_ref[...] = _layer_norm_rows(v, g_ref[...], b_ref[...])


def _outproj_ln_call(a, w, h, modt, ln_g, ln_b, *, alpha, seq, ctx_row, tm):
    t, k = a.shape
    d = w.shape[1]
    return pl.pallas_call(
        functools.partial(_outproj_ln_kernel, alpha=alpha),
        grid=(t // tm,),
        in_specs=[
            pl.BlockSpec((tm, k), lambda i: (i, 0)),
            pl.BlockSpec((k, d), lambda i: (0, 0)),
            pl.BlockSpec((tm, d), lambda i: (i, 0)),
            pl.BlockSpec((1, 1, d), _mod_index(2, tm, seq, ctx_row)),
            pl.BlockSpec((1, d), lambda i: (0, 0)),
            pl.BlockSpec((1, d), lambda i: (0, 0)),
        ],
        out_specs=pl.BlockSpec((tm, d), lambda i: (i, 0)),
        out_shape=jax.ShapeDtypeStruct((t, d), F32),
        compiler_params=_cparams(("arbitrary",)),
        name="outproj_ln",
    )(a, w, h, modt, ln_g.reshape(1, d), ln_b.reshape(1, d))


def _mlp_ln_kernel(h_ref, sh_ref, sc_ref, gate_ref, w1_ref, w2_ref, g_ref, b_ref, o_ref, u_scr, acc_scr, *, alpha):
    j = pl.program_id(1)

    @pl.when(j == 0)
    def _():
        u_scr[...] = (h_ref[...] * (1.0 + sc_ref[0]) + sh_ref[0]).astype(BF16)
        acc_scr[...] = jnp.zeros_like(acc_scr)

    a = jnp.maximum(_dot(u_scr[...], w1_ref[...]), 0.0)
    acc_scr[...] += _dot((a * a).astype(BF16), w2_ref[...])

    @pl.when(j == pl.num_programs(1) - 1)
    def _():
        v = alpha * h_ref[...] + gate_ref[0] * acc_scr[...]
        o_ref[...] = _layer_norm_rows(v, g_ref[...], b_ref[...])


def _mlp_ln_call(h, modt, w1, w2, ln_g, ln_b, *, alpha, seq, ctx_row, tm, tf):
    t, d = h.shape
    ff = w1.shape[1]
    return pl.pallas_call(
        functools.partial(_mlp_ln_kernel, alpha=alpha),
        grid=(t // tm, ff // tf),
        in_specs=[
            pl.BlockSpec((tm, d), lambda i, j: (i, 0)),
            pl.BlockSpec((1, 1, d), _mod_index(3, tm, seq, ctx_row)),
            pl.BlockSpec((1, 1, d), _mod_index(4, tm, seq, ctx_row)),
            pl.BlockSpec((1, 1, d), _mod_index(5, tm, seq, ctx_row)),
            pl.BlockSpec((d, tf), lambda i, j: (0, j)),
            pl.BlockSpec((tf, d), lambda i, j: (j, 0)),
            pl.BlockSpec((1, d), lambda i, j: (0, 0)),
            pl.BlockSpec((1, d), lambda i, j: (0, 0)),
        ],
        out_specs=pl.BlockSpec((tm, d), lambda i, j: (i, 0)),
        out_shape=jax.ShapeDtypeStruct((t, d), F32),
        scratch_shapes=[pltpu.VMEM((tm, d), BF16), pltpu.VMEM((tm, d), F32)],
        compiler_params=_cparams(("arbitrary", "arbitrary")),
        name="mlp_ln",
    )(h, modt, modt, modt, w1, w2, ln_g.reshape(1, d), ln_b.reshape(1, d))


def _ssm_inproj_kernel(h_ref, sh_ref, sc_ref, w_ref, wdt_ref, dtb_ref, o_ref, dt_ref, u_scr, *, n_dt):
    j = pl.program_id(1)

    @pl.when(j == 0)
    def _():
        u = (h_ref[...] * (1.0 + sc_ref[0]) + sh_ref[0]).astype(BF16)
        u_scr[...] = u
        raw = _dot(u, wdt_ref[...]) + dtb_ref[...]
        sp = jnp.maximum(raw, 0.0) + jnp.log1p(jnp.exp(-jnp.abs(raw)))
        lane = lax.broadcasted_iota(jnp.int32, sp.shape, 1)
        dt_ref[...] = jnp.where(lane < n_dt, sp, 0.0)

    res = _dot(u_scr[...], w_ref[...]).astype(BF16)
    tile = o_ref.shape[2]
    for k in range(o_ref.shape[0]):
        o_ref[k] = res[:, k * tile:(k + 1) * tile]


def _ssm_inproj_call(h, modt, w_main, w_dt, dt_bias, *, n_dt, seq, ctx_row, tm, tn, tile):
    t, d = h.shape
    nstep = w_main.shape[1] // tn
    per_step = tn // tile
    return pl.pallas_call(
        functools.partial(_ssm_inproj_kernel, n_dt=n_dt),
        grid=(t // tm, nstep),
        in_specs=[
            pl.BlockSpec((tm, d), lambda i, j: (i, 0)),
            pl.BlockSpec((1, 1, d), _mod_index(0, tm, seq, ctx_row)),
            pl.BlockSpec((1, 1, d), _mod_index(1, tm, seq, ctx_row)),
            pl.BlockSpec((d, tn), lambda i, j: (0, j)),
            pl.BlockSpec((d, LANES), lambda i, j: (0, 0)),
            pl.BlockSpec((1, LANES), lambda i, j: (0, 0)),
        ],
        out_specs=[
            pl.BlockSpec((per_step, tm, tile), lambda i, j: (j, i, 0)),
            pl.BlockSpec((tm, LANES), lambda i, j: (i, 0)),
        ],
        out_shape=[
            jax.ShapeDtypeStruct((nstep * per_step, t, tile), BF16),
            jax.ShapeDtypeStruct((t, LANES), F32),
        ],
        scratch_shapes=[pltpu.VMEM((tm, d), BF16)],
        compiler_params=_cparams(("arbitrary", "arbitrary")),
        name="ssm_inproj",
    )(h, modt, modt, w_main, w_dt, dt_bias)


def _conv_kernel(prev_ref, cur_ref, next_ref, w_ref, b_ref, o_ref, *, nt):
    i = pl.program_id(2)
    tl = cur_ref.shape[1]
    halo = prev_ref.shape[1]
    prev = jnp.where(i > 0, prev_ref[0].astype(F32), 0.0)
    nxt = jnp.where(i < nt - 1, next_ref[0].astype(F32), 0.0)
    xx = jnp.concatenate([prev, cur_ref[0].astype(F32), nxt], axis=0)
    total = tl + 2 * halo
    w = w_ref[0]
    acc = jnp.zeros((tl, xx.shape[1]), F32) + b_ref[0]
    for k in range(SSM_CONV_W):
        shift = SSM_CONV_W // 2 - k
        rolled = xx if shift == 0 else pltpu.roll(xx, shift % total, 0)
        acc = acc + w[k:k + 1] * rolled[halo:halo + tl]
    o_ref[0] = (acc * _sigmoid(acc)).astype(BF16)


def _conv_call(zxbc, conv_w, conv_b, *, bsz, seq, first_tile, tl):
    ntile_all, t, tn = zxbc.shape
    ntile = ntile_all - first_tile
    nt = seq // tl
    halo = BF16_SUBLANES
    hb = tl // halo
    last_h = t // halo - 1
    return pl.pallas_call(
        functools.partial(_conv_kernel, nt=nt),
        grid=(ntile, bsz, nt),
        in_specs=[
            pl.BlockSpec((1, halo, tn), lambda c, b, i: (first_tile + c, jnp.maximum((b * nt + i) * hb - 1, 0), 0)),
            pl.BlockSpec((1, tl, tn), lambda c, b, i: (first_tile + c, b * nt + i, 0)),
            pl.BlockSpec((1, halo, tn), lambda c, b, i: (first_tile + c, jnp.minimum((b * nt + i + 1) * hb, last_h), 0)),
            pl.BlockSpec((1, 8, tn), lambda c, b, i: (c, 0, 0)),
            pl.BlockSpec((1, 1, tn), lambda c, b, i: (c, 0, 0)),
        ],
        out_specs=pl.BlockSpec((1, tl, tn), lambda c, b, i: (c, b * nt + i, 0)),
        out_shape=jax.ShapeDtypeStruct((ntile, t, tn), BF16),
        compiler_params=_cparams(("arbitrary", "arbitrary", "arbitrary")),
        name="ssm_conv",
    )(zxbc, zxbc, zxbc, conv_w, conv_b)


HEADS_PER_GROUP = 8
DT_LANES_PER_GROUP = 2 * HEADS_PER_GROUP


def _expand_heads(v, lane0, nrows):
    rows = max(nrows, 8)
    if rows != nrows:
        v = jnp.broadcast_to(v, (rows, LANES))
    lane = lax.broadcasted_iota(jnp.int32, (rows, LANES), 1)
    parts = []
    for p in range(HEADS_PER_GROUP // 2):
        idx = lane0 + 2 * p + (lane >> 6)
        parts.append(jnp.take_along_axis(v, idx, axis=1, mode="promise_in_bounds"))
    return jnp.concatenate(parts, axis=1)[:nrows]


def _cumsum_rows(a):
    q = a.shape[0]
    tril = (lax.broadcasted_iota(jnp.int32, (q, q), 0) >= lax.broadcasted_iota(jnp.int32, (q, q), 1))
    tril = jnp.where(tril, 1.0, 0.0).astype(BF16)
    hi = a.astype(BF16)
    r1 = a - hi.astype(F32)
    mid = r1.astype(BF16)
    lo = (r1 - mid.astype(F32)).astype(BF16)
    return _dot(tril, hi) + _dot(tril, mid) + _dot(tril, lo)


def _chunk_terms(dt, a_neg):
    q = dt.shape[0]
    lane = lax.broadcasted_iota(jnp.int32, (q, LANES), 1)
    is_bwd = (lane & HEADS_PER_GROUP) != 0
    a = dt * a_neg
    acs = _cumsum_rows(a)
    tot = acs[q - 1:q, :]
    e = jnp.where(is_bwd, acs - a, acs)
    w_state = dt * jnp.exp(jnp.where(is_bwd, e, tot - acs))
    f_out = jnp.exp(jnp.where(is_bwd, tot - e, acs))
    dec = jnp.exp(tot)
    return e, w_state, f_out, dec


def _state_update(s_scr, g, x, bm, w_state, dec, lane0):
    q = x.shape[0]
    xw = (x.astype(F32) * _expand_heads(w_state, lane0, q)).astype(BF16)
    s_scr[g] = s_scr[g] * _expand_heads(dec, lane0, 1) + _dot_tn(bm, xw)


def _ssd_kernel(alog_ref, d_ref, ng_ref, xc_ref, bc_ref, dtc_ref, xl_ref, bl_ref, cl_ref, zl_ref, dtl_ref,
                y_ref, sf_scr, sb_scr, sbs_scr, *, ncc, ncl):
    s = pl.program_id(1)
    a_neg = -jnp.exp(alog_ref[...])
    q = SSM_CHUNK
    n = SSM_STATE
    gcols = HEADS_PER_GROUP * SSM_HEAD_DIM

    def sweep(s_scr, x_ref, b_ref, dt_ref, dir_off, keep=None):
        _, w_state, _, dec = _chunk_terms(dt_ref[...], a_neg)
        for g in range(SSM_GROUPS):
            if keep is not None:
                sbs_scr[keep, g] = s_scr[g].astype(BF16)
            _state_update(s_scr, g, x_ref[g], b_ref[0, :, g * n:(g + 1) * n], w_state, dec,
                          g * DT_LANES_PER_GROUP + dir_off)

    @pl.when(s == 0)
    def _():
        sf_scr[...] = jnp.zeros_like(sf_scr)
        sb_scr[...] = jnp.zeros_like(sb_scr)

    @pl.when(s < ncc)
    def _():
        sweep(sb_scr, xc_ref, bc_ref, dtc_ref, HEADS_PER_GROUP)

    @pl.when((s >= ncc) & (s < 2 * ncc))
    def _():
        sweep(sf_scr, xc_ref, bc_ref, dtc_ref, 0)

    @pl.when((s >= 2 * ncc) & (s < 2 * ncc + ncl))
    def _():
        sweep(sb_scr, xl_ref, bl_ref, dtl_ref, HEADS_PER_GROUP, keep=ncl - 1 - (s - 2 * ncc))

    @pl.when(s >= 2 * ncc + ncl)
    def _():
        lc = s - 2 * ncc - ncl
        dt = dtl_ref[...]
        e, w_state, f_out, dec = _chunk_terms(dt, a_neg)
        e_t = e.T
        dt_t = dt.T
        li = lax.broadcasted_iota(jnp.int32, (q, q), 0)
        si = lax.broadcasted_iota(jnp.int32, (q, q), 1)
        lower = li >= si
        upper = li <= si
        first = lax.broadcasted_iota(jnp.int32, (q, LANES), 1) < SSM_HEAD_DIM
        for g in range(SSM_GROUPS):
            lf = g * DT_LANES_PER_GROUP
            lb = lf + HEADS_PER_GROUP
            x = xl_ref[g]
            bm = bl_ref[0, :, g * n:(g + 1) * n]
            cm = cl_ref[0, :, g * n:(g + 1) * n]
            cb = _dot_nt(cm, bm)
            y_f = _dot(cm, sf_scr[g].astype(BF16))
            y_b = _dot(cm, sbs_scr[lc, g])
            y_parts = []
            for p in range(HEADS_PER_GROUP // 2):
                ms = []
                for r in (2 * p, 2 * p + 1):
                    col_f = jnp.broadcast_to(e[:, lf + r:lf + r + 1], (q, q))
                    col_b = jnp.broadcast_to(e[:, lb + r:lb + r + 1], (q, q))
                    arg = jnp.where(lower, col_f - e_t[lf + r:lf + r + 1, :], e_t[lb + r:lb + r + 1, :] - col_b)
                    dsel = (jnp.where(lower, dt_t[lf + r:lf + r + 1, :], 0.0)
                            + jnp.where(upper, dt_t[lb + r:lb + r + 1, :], 0.0))
                    ms.append((cb * jnp.exp(arg) * dsel).astype(BF16))
                xp = x[:, p * LANES:(p + 1) * LANES]
                zero = jnp.zeros_like(xp)
                x_bd = jnp.concatenate([jnp.where(first, xp, zero), jnp.where(first, zero, xp)], axis=0)
                y_parts.append(_dot(jnp.concatenate(ms, axis=1), x_bd))
            cols = slice(g * gcols, (g + 1) * gcols)
            y = (jnp.concatenate(y_parts, axis=1)
                 + y_f * _expand_heads(f_out, lf, q) + y_b * _expand_heads(f_out, lb, q)
                 + x.astype(F32) * d_ref[:, cols])
            z = zl_ref[g].astype(F32)
            yy = y * (z * _sigmoid(z))
            ms_ = jnp.mean(yy * yy, axis=-1, keepdims=True)
            y_ref[:, cols] = (yy * lax.rsqrt(ms_ + RMS_EPS) * ng_ref[:, cols]).astype(BF16)
            _state_update(sf_scr, g, x, bm, w_state, dec, lf)


def _ssd_call(a_log, d_row, norm_g, xbc_c, dt_c, zx_l, xbc_l, dt_l, *, bsz, n_ctx, seq):
    q = SSM_CHUNK
    ncc = n_ctx // q
    ncl = seq // q
    gcols = zx_l.shape[2]
    d_inner = SSM_GROUPS * gcols
    nsteps = 2 * ncc + 2 * ncl
    n_state = SSM_STATE
    xt = SSM_GROUPS

    def cchunk(s):
        return jnp.where(s < ncc, ncc - 1 - s, jnp.where(s < 2 * ncc, s - ncc, ncc - 1))

    def lchunk(s):
        t = s - 2 * ncc
        return jnp.where(t < 0, ncl - 1, jnp.where(t < ncl, ncl - 1 - t, t - ncl))

    def lchunk_fwd(s):
        return jnp.maximum(s - 2 * ncc - ncl, 0)

    in_specs = [
        pl.BlockSpec((1, LANES), lambda b, s: (0, 0)),
        pl.BlockSpec((1, d_inner), lambda b, s: (0, 0)),
        pl.BlockSpec((1, d_inner), lambda b, s: (0, 0)),
        pl.BlockSpec((SSM_GROUPS, q, gcols), lambda b, s: (0, b * ncc + cchunk(s), 0)),
        pl.BlockSpec((1, q, SSM_GROUPS * n_state), lambda b, s: (xt, b * ncc + cchunk(s), 0)),
        pl.BlockSpec((q, LANES), lambda b, s: (b * ncc + cchunk(s), 0)),
        pl.BlockSpec((SSM_GROUPS, q, gcols), lambda b, s: (0, b * ncl + lchunk(s), 0)),
        pl.BlockSpec((1, q, SSM_GROUPS * n_state), lambda b, s: (xt, b * ncl + lchunk(s), 0)),
        pl.BlockSpec((1, q, SSM_GROUPS * n_state), lambda b, s: (xt + 1, b * ncl + lchunk_fwd(s), 0)),
        pl.BlockSpec((SSM_GROUPS, q, gcols), lambda b, s: (0, b * ncl + lchunk_fwd(s), 0)),
        pl.BlockSpec((q, LANES), lambda b, s: (b * ncl + lchunk(s), 0)),
    ]
    return pl.pallas_call(
        functools.partial(_ssd_kernel, ncc=ncc, ncl=ncl),
        grid=(bsz, nsteps),
        in_specs=in_specs,
        out_specs=pl.BlockSpec((q, d_inner), lambda b, s: (b * ncl + lchunk_fwd(s), 0)),
        out_shape=jax.ShapeDtypeStruct((bsz * seq, d_inner), BF16),
        scratch_shapes=[
            pltpu.VMEM((SSM_GROUPS, n_state, gcols), F32),
            pltpu.VMEM((SSM_GROUPS, n_state, gcols), F32),
            pltpu.VMEM((ncl, SSM_GROUPS, n_state, gcols), BF16),
        ],
        compiler_params=_cparams(("arbitrary", "arbitrary")),
        name="ssd_scan",
    )(a_log, d_row, norm_g, xbc_c, xbc_c, dt_c, xbc_l, xbc_l, xbc_l, zx_l, dt_l)


def _rope_tables(seq):
    rows = seq // GRID_W
    row = jnp.repeat(jnp.arange(rows), GRID_W).astype(F32)
    col = jnp.tile(jnp.arange(GRID_W), rows).astype(F32)
    inv_freq = ROPE_BASE ** (-jnp.arange(0, ROPE_AXIS_DIM, 2, dtype=F32) / ROPE_AXIS_DIM)
    ang_r = row[:, None] * inv_freq[None, :]
    ang_c = col[:, None] * inv_freq[None, :]
    cos = jnp.concatenate([jnp.cos(ang_r)] * 2 + [jnp.cos(ang_c)] * 2, axis=-1)
    sin = jnp.concatenate([-jnp.sin(ang_r), jnp.sin(ang_r), -jnp.sin(ang_c), jnp.sin(ang_c)], axis=-1)
    reps = LANES // HEAD_DIM
    return jnp.tile(cos, (1, reps)), jnp.tile(sin, (1, reps))


def _dup_heads(w):
    d, n = w.shape
    w = w.reshape(d, n // HEAD_DIM, 1, HEAD_DIM)
    return jnp.broadcast_to(w, (d, n // HEAD_DIM, LANES // HEAD_DIM, HEAD_DIM)).reshape(d, -1)


def _attention_layer(h_lat, h_ctx, modt, w_in, w_out, sink, ln_g, ln_b, *, bsz, seq, n_ctx, ctx_row, alpha, tm):
    nq = ATT_HEADS * HEAD_DIM
    nk = ATT_KV_HEADS * HEAD_DIM
    wq, wk, wv = w_in[:, :nq], w_in[:, nq:nq + nk], w_in[:, nq + nk:]
    wk, wv = _dup_heads(wk), _dup_heads(wv)
    w_cat = jnp.concatenate([wq, wk, wv], axis=1).astype(BF16)
    nrot = nq + wk.shape[1]
    tables = _rope_tables(seq)
    qkv_l, rot_l = _att_inproj_call(h_lat, modt, w_cat, tables, seq=seq, ctx_row=None, nq=nq, nrot=nrot, tm=tm)
    (qkv_c,) = _att_inproj_call(h_ctx, modt, w_cat, None, seq=n_ctx, ctx_row=ctx_row, nq=nq, nrot=nrot,
                                tm=min(tm, h_ctx.shape[0]))
    o_l = _attn_call(sink, qkv_l, rot_l, qkv_c, bsz=bsz, seq=seq, n_ctx=n_ctx, window=True)
    o_c = _attn_call(sink, qkv_c, None, qkv_c, bsz=bsz, seq=n_ctx, n_ctx=n_ctx, window=False)
    w_o = w_out.astype(BF16)
    h_lat = _outproj_ln_call(o_l, w_o, h_lat, modt, ln_g, ln_b, alpha=alpha, seq=seq, ctx_row=None, tm=tm)
    h_ctx = _outproj_ln_call(o_c, w_o, h_ctx, modt, ln_g, ln_b, alpha=alpha, seq=n_ctx, ctx_row=ctx_row,
                             tm=min(tm, h_ctx.shape[0]))
    return h_lat, h_ctx


def _ssm_layer(h_lat, h_ctx, modt, w_in, conv_w, conv_b, dt_bias, a_log, d_skip, norm_g, w_out, ln_g, ln_b,
               *, bsz, seq, n_ctx, ctx_row, alpha, tm, tm_big):
    d_inner = w_out.shape[0]
    heads = d_inner // SSM_HEAD_DIM
    gcols = d_inner // SSM_GROUPS
    hpg = heads // SSM_GROUPS
    conv_dim = d_inner + 2 * SSM_GROUPS * SSM_STATE
    n_main = d_inner + conv_dim
    assert hpg == HEADS_PER_GROUP and gcols == SSM_GROUPS * SSM_STATE
    w_main = w_in[:, :n_main].astype(BF16)

    def regroup(v):
        lead = v.shape[:-2]
        v = v.reshape(lead + (2, SSM_GROUPS, hpg))
        v = jnp.moveaxis(v, -3, -2).reshape(lead + (SSM_GROUPS, 2 * hpg))
        return v

    w_dt = regroup(w_in[:, n_main:].reshape(-1, 2, heads)).reshape(-1, 2 * heads)
    w_dt = jnp.pad(w_dt, ((0, 0), (0, LANES - 2 * heads))).astype(BF16)
    dtb = jnp.pad(regroup(dt_bias).reshape(1, 2 * heads), ((0, 0), (0, LANES - 2 * heads)))
    a_log_row = jnp.pad(regroup(a_log).reshape(1, 2 * heads), ((0, 0), (0, LANES - 2 * heads)))
    d_row = jnp.repeat(d_skip, SSM_HEAD_DIM).reshape(1, d_inner)
    ng = norm_g.reshape(1, d_inner)
    ntile_conv = conv_dim // gcols
    cw = jnp.pad(conv_w, ((0, 8 - SSM_CONV_W), (0, 0))).reshape(8, ntile_conv, gcols).transpose(1, 0, 2)
    cb = conv_b.reshape(ntile_conv, 1, gcols)

    zx_l, dt_l = _ssm_inproj_call(h_lat, modt, w_main, w_dt, dtb, n_dt=2 * heads, seq=seq, ctx_row=None,
                                  tm=tm_big, tn=2 * gcols, tile=gcols)
    zx_c, dt_c = _ssm_inproj_call(h_ctx, modt, w_main, w_dt, dtb, n_dt=2 * heads, seq=n_ctx, ctx_row=ctx_row,
                                  tm=min(tm_big, h_ctx.shape[0]), tn=2 * gcols, tile=gcols)
    first = d_inner // gcols
    xbc_l = _conv_call(zx_l, cw, cb, bsz=bsz, seq=seq, first_tile=first, tl=min(512, seq))
    xbc_c = _conv_call(zx_c, cw, cb, bsz=bsz, seq=n_ctx, first_tile=first, tl=min(512, n_ctx))
    y = _ssd_call(a_log_row, d_row, ng, xbc_c, dt_c, zx_l, xbc_l, dt_l, bsz=bsz, n_ctx=n_ctx, seq=seq)
    return _outproj_ln_call(y, w_out.astype(BF16), h_lat, modt, ln_g, ln_b, alpha=alpha, seq=seq, ctx_row=None, tm=tm)


def kernel(x, c, ctx, c_ctx, w_mod, b_mod, ln_mix_g, ln_mix_b, ln_ff_g, ln_ff_b, att_w_in, att_w_out, att_sink,
           ssm_w_in, ssm_conv_w, ssm_conv_b, ssm_dt_bias, ssm_a_log, ssm_d, ssm_norm_g, ssm_w_out, ff_w1, ff_w2):
    bsz, seq, d = x.shape
    n_ctx = ctx.shape[1]
    depth = w_mod.shape[0]
    alpha = (2.0 * depth) ** 0.25
    tm = 512
    tm_big = 1024 if seq % 1024 == 0 and (bsz * n_ctx) % 1024 == 0 else tm
    ctx_row = bsz
    rows = -(-(bsz + 1) // MOD_ROWS_ALIGN) * MOD_ROWS_ALIGN
    cc = jnp.concatenate([c, c_ctx[None, :], jnp.zeros((rows - bsz - 1, d), F32)], axis=0)
    mod = _mod_call(cc, w_mod, b_mod)
    h_lat = x.reshape(bsz * seq, d)
    h_ctx = ctx.reshape(bsz * n_ctx, d)
    for i in range(depth):
        last = i == depth - 1
        j = i // N_MIXERS
        modt = mod[i].reshape(rows * N_MOD, 1, d)
        if i % N_MIXERS == 0:
            h_lat, h_ctx_mix = _attention_layer(h_lat, h_ctx, modt, att_w_in[j], att_w_out[j], att_sink[j],
                                                ln_mix_g[i], ln_mix_b[i], bsz=bsz, seq=seq, n_ctx=n_ctx,
                                                ctx_row=ctx_row, alpha=alpha, tm=tm)
        else:
            if not last:
                raise NotImplementedError("context outputs of an SSD layer are only needed when it is not last")
            h_lat = _ssm_layer(h_lat, h_ctx, modt, ssm_w_in[j], ssm_conv_w[j], ssm_conv_b[j], ssm_dt_bias[j],
                               ssm_a_log[j], ssm_d[j], ssm_norm_g[j], ssm_w_out[j], ln_mix_g[i], ln_mix_b[i],
                               bsz=bsz, seq=seq, n_ctx=n_ctx, ctx_row=ctx_row, alpha=alpha, tm=tm, tm_big=tm_big)
            h_ctx_mix = None
        w1 = ff_w1[i].astype(BF16)
        w2 = ff_w2[i].astype(BF16)
        h_lat = _mlp_ln_call(h_lat, modt, w1, w2, ln_ff_g[i], ln_ff_b[i], alpha=alpha, seq=seq, ctx_row=None,
                             tm=tm_big, tf=1024)
        if not last:
            h_ctx = _mlp_ln_call(h_ctx_mix, modt, w1, w2, ln_ff_g[i], ln_ff_b[i], alpha=alpha, seq=n_ctx,
                                 ctx_row=ctx_row, tm=min(tm_big, h_ctx_mix.shape[0]), tf=1024)
    return h_lat.reshape(bsz, seq, d)
```

```python
import functools

import jax
import jax.numpy as jnp
from jax import lax
from jax.experimental import pallas as pl
from jax.experimental.pallas import tpu as pltpu

F32 = jnp.float32
BF16 = jnp.bfloat16

GRID_W = 64
N_MIXERS = 2
ATT_HEADS = 16
ATT_KV_HEADS = 4
HEAD_DIM = 64
ATT_GROUP = ATT_HEADS // ATT_KV_HEADS
WINDOW = 128
BLOCK = 128
ROPE_BASE = 10000.0
ROPE_AXIS_DIM = HEAD_DIM // 2

SSM_HEAD_DIM = 64
SSM_GROUPS = 4
SSM_STATE = 128
SSM_CONV_W = 5
SSM_CHUNK = 128

LOG2E = 1.4426950408889634
Q_SCALE = HEAD_DIM ** -0.5 * LOG2E

N_MOD = 6
LN_EPS = 1e-5
RMS_EPS = 1e-5
NEG_INF = -1e30

LANES = 128
MOD_ROWS_ALIGN = 16
ROW_SPLIT = 4
VMEM_LIMIT = 48 * 1024 * 1024


def _cparams(sem):
    return pltpu.CompilerParams(dimension_semantics=sem, vmem_limit_bytes=VMEM_LIMIT)


def _sigmoid(x):
    return 1.0 / (1.0 + jnp.exp(-x))


def _dot(a, b):
    return jnp.dot(a, b, preferred_element_type=F32)


def _dot_nt(a, b):
    return lax.dot_general(a, b, (((1,), (1,)), ((), ())), preferred_element_type=F32)


def _dot_tn(a, b):
    return lax.dot_general(a, b, (((0,), (0,)), ((), ())), preferred_element_type=F32)


def _layer_norm_rows(v, g, b):
    mu = jnp.mean(v, axis=-1, keepdims=True)
    d = v - mu
    var = jnp.mean(d * d, axis=-1, keepdims=True)
    return d * lax.rsqrt(var + LN_EPS) * g + b


def _mod_kernel(c_ref, w_ref, b_ref, o_ref):
    c = c_ref[...]
    s = (c * _sigmoid(c)).astype(BF16)
    o_ref[0] = _dot(s, w_ref[0].astype(BF16)) + b_ref[0]


def _mod_call(cc, w_mod, b_mod):
    depth, d, n = w_mod.shape
    rows = cc.shape[0]
    tn = 1024
    return pl.pallas_call(
        _mod_kernel,
        grid=(depth, n // tn),
        in_specs=[
            pl.BlockSpec((rows, d), lambda l, j: (0, 0)),
            pl.BlockSpec((1, d, tn), lambda l, j: (l, 0, j)),
            pl.BlockSpec((1, 1, tn), lambda l, j: (l, 0, j)),
        ],
        out_specs=pl.BlockSpec((1, rows, tn), lambda l, j: (l, 0, j)),
        out_shape=jax.ShapeDtypeStruct((depth, rows, n), F32),
        compiler_params=_cparams(("arbitrary", "arbitrary")),
        name="adaln_mod",
    )(cc, w_mod, b_mod.reshape(depth, 1, n))


def _mod_index(which, tm, seq, ctx_row):
    if ctx_row is None:
        return lambda i, *_: ((i * tm // seq) * N_MOD + which, 0, 0)
    return lambda i, *_: (ctx_row * N_MOD + which, 0, 0)


def _att_inproj_kernel(*refs, rope, nq):
    if rope:
        x_ref, sh_ref, sc_ref, w_ref, cos_ref, sin_ref, qkv_ref, rot_ref = refs
    else:
        x_ref, sh_ref, sc_ref, w_ref, qkv_ref = refs
    u = (x_ref[...] * (1.0 + sc_ref[0]) + sh_ref[0]).astype(BF16)
    acc = _dot(u, w_ref[...])
    n = acc.shape[1]
    nrot = rot_ref.shape[1] if rope else 0
    if rope:
        cos = cos_ref[...]
        sin = sin_ref[...]
        lane = lax.broadcasted_iota(jnp.int32, cos.shape, 1)
        first_half = (lane & (ROPE_AXIS_DIM - 1)) < (ROPE_AXIS_DIM // 2)
    for cidx in range(n // LANES):
        chunk = acc[:, cidx * LANES:(cidx + 1) * LANES]
        if cidx * LANES < nq:
            chunk = chunk * Q_SCALE
        qkv_ref[:, cidx * LANES:(cidx + 1) * LANES] = chunk.astype(BF16)
        if rope and cidx * LANES < nrot:
            half = ROPE_AXIS_DIM // 2
            swapped = jnp.where(first_half, pltpu.roll(chunk, LANES - half, 1), pltpu.roll(chunk, half, 1))
            rot_ref[:, cidx * LANES:(cidx + 1) * LANES] = (chunk * cos + swapped * sin).astype(BF16)


def _att_inproj_call(h, modt, w, tables, *, seq, ctx_row, nq, nrot, tm):
    t, d = h.shape
    n = w.shape[1]
    rope = tables is not None
    in_specs = [
        pl.BlockSpec((tm, d), lambda i: (i, 0)),
        pl.BlockSpec((1, 1, d), _mod_index(0, tm, seq, ctx_row)),
        pl.BlockSpec((1, 1, d), _mod_index(1, tm, seq, ctx_row)),
        pl.BlockSpec((d, n), lambda i: (0, 0)),
    ]
    args = [h, modt, modt, w]
    out_specs = [pl.BlockSpec((tm, n), lambda i: (i, 0))]
    out_shape = [jax.ShapeDtypeStruct((t, n), BF16)]
    if rope:
        per_seq = seq // tm
        in_specs += [pl.BlockSpec((tm, LANES), lambda i: (i % per_seq, 0))] * 2
        args += list(tables)
        out_specs.append(pl.BlockSpec((tm, nrot), lambda i: (i, 0)))
        out_shape.append(jax.ShapeDtypeStruct((t, nrot), BF16))
    return pl.pallas_call(
        functools.partial(_att_inproj_kernel, rope=rope, nq=nq),
        grid=(t // tm,),
        in_specs=in_specs,
        out_specs=out_specs,
        out_shape=out_shape,
        compiler_params=_cparams(("arbitrary",)),
        name="att_inproj_rope" if rope else "att_inproj",
    )(*args)


def _attn_kernel(*refs, window, nb):
    if window:
        (sink_ref, qr_ref, qp_ref, kp_ref, kc_ref, kn_ref, vp_ref, vc_ref, vn_ref,
         kx_ref, vx_ref, o_ref) = refs
    else:
        sink_ref, qp_ref, kx_ref, vx_ref, o_ref = refs
    i = pl.program_id(1)
    rows = ATT_GROUP * BLOCK
    half_lane = lax.broadcasted_iota(jnp.int32, (BLOCK, LANES), 1) < HEAD_DIM
    ridx = lax.broadcasted_iota(jnp.int32, (rows, 1), 0)
    if window:
        li = lax.broadcasted_iota(jnp.int32, (BLOCK, BLOCK), 0)
        ci = lax.broadcasted_iota(jnp.int32, (BLOCK, BLOCK), 1)
        ok_prev = ci >= li + jnp.where(i > 0, 0, BLOCK)
        ok_next = ci + jnp.where(i < nb - 1, 0, BLOCK) <= li

    def masked(s_blk, ok):
        return jnp.concatenate([jnp.where(ok, s_blk[r * BLOCK:(r + 1) * BLOCK], NEG_INF)
                                for r in range(ATT_GROUP)], axis=0)

    def with_ones(v):
        return jnp.concatenate([v, jnp.ones_like(v)], axis=1)

    def stack_heads(q_ref, g):
        parts = []
        for p in range(ATT_GROUP // 2):
            c0 = (g * (ATT_GROUP // 2) + p) * LANES
            pair = q_ref[0, :, c0:c0 + LANES]
            zero = jnp.zeros_like(pair)
            parts.append(jnp.where(half_lane, pair, zero))
            parts.append(jnp.where(half_lane, zero, pair))
        return jnp.concatenate(parts, axis=0)

    def scores(g):
        gl = slice(g * LANES, (g + 1) * LANES)
        s_ctx = _dot_nt(stack_heads(qp_ref, g), kx_ref[0, :, gl])
        if not window:
            return s_ctx, None
        kw = jnp.concatenate([kp_ref[0, :, gl], kc_ref[0, :, gl], kn_ref[0, :, gl]], axis=0)
        s_win = _dot_nt(stack_heads(qr_ref, g), kw)
        s_win = jnp.concatenate([masked(s_win[:, :BLOCK], ok_prev), s_win[:, BLOCK:2 * BLOCK],
                                 masked(s_win[:, 2 * BLOCK:], ok_next)], axis=1)
        return s_ctx, s_win

    ahead = scores(0)
    for g in range(ATT_KV_HEADS):
        s_ctx, s_win = ahead
        if g + 1 < ATT_KV_HEADS:
            ahead = scores(g + 1)
        gl = slice(g * LANES, (g + 1) * LANES)
        sink_col = jnp.full((rows, 1), sink_ref[g * ATT_GROUP + ATT_GROUP - 1] * LOG2E, F32)
        for r in range(ATT_GROUP - 2, -1, -1):
            sink_col = jnp.where(ridx < (r + 1) * BLOCK, sink_ref[g * ATT_GROUP + r] * LOG2E, sink_col)
        pieces = [s_ctx[:, k * LANES:(k + 1) * LANES] for k in range(s_ctx.shape[1] // LANES)]
        if window:
            pieces += [s_win[:, k * BLOCK:(k + 1) * BLOCK] for k in range(3)]
        mx = pieces[0]
        for piece in pieces[1:]:
            mx = jnp.maximum(mx, piece)
        m = jnp.maximum(jnp.max(mx, axis=-1, keepdims=True), sink_col)
        p_ctx = jnp.exp2(s_ctx - m)
        if window:
            vw = jnp.concatenate([vp_ref[0, :, gl], vc_ref[0, :, gl], vn_ref[0, :, gl]], axis=0)
            acc = (_dot(p_ctx.astype(BF16), with_ones(vx_ref[0, :, gl]))
                   + _dot(jnp.exp2(s_win - m).astype(BF16), with_ones(vw)))
            denom = acc[:, LANES:LANES + 1] + jnp.exp2(sink_col - m)
            o = acc[:, :LANES] * (1.0 / denom)
        else:
            denom = jnp.sum(p_ctx, axis=-1, keepdims=True) + jnp.exp2(sink_col - m)
            o = _dot(p_ctx.astype(BF16), vx_ref[0, :, gl]) * (1.0 / denom)
        for p in range(ATT_GROUP // 2):
            a = o[(2 * p) * BLOCK:(2 * p + 1) * BLOCK]
            b = o[(2 * p + 1) * BLOCK:(2 * p + 2) * BLOCK]
            c0 = (g * (ATT_GROUP // 2) + p) * LANES
            o_ref[0, :, c0:c0 + LANES] = jnp.where(half_lane, a, b).astype(BF16)


def _attn_call(sink, qkv, rot, qkv_ctx, *, bsz, seq, n_ctx, window):
    nq = ATT_HEADS * HEAD_DIM
    nkv = ATT_KV_HEADS * LANES
    nb = seq // BLOCK
    qkv3 = qkv.reshape(bsz, seq, qkv.shape[1])
    ctx3 = qkv_ctx.reshape(bsz, n_ctx, qkv_ctx.shape[1])
    kcol = nq // nkv
    vcol = kcol + 1
    smem = pl.BlockSpec(memory_space=pltpu.SMEM)
    qspec = pl.BlockSpec((1, BLOCK, nq), lambda b, i: (b, i, 0))
    kx = pl.BlockSpec((1, n_ctx, nkv), lambda b, i: (b, 0, kcol))
    vx = pl.BlockSpec((1, n_ctx, nkv), lambda b, i: (b, 0, vcol))
    if window:
        rot3 = rot.reshape(bsz, seq, rot.shape[1])

        def kv_specs(colblk):
            return [
                pl.BlockSpec((1, BLOCK, nkv), lambda b, i: (b, jnp.maximum(i - 1, 0), colblk)),
                pl.BlockSpec((1, BLOCK, nkv), lambda b, i: (b, i, colblk)),
                pl.BlockSpec((1, BLOCK, nkv), lambda b, i: (b, jnp.minimum(i + 1, nb - 1), colblk)),
            ]
        in_specs = [smem, qspec, qspec] + kv_specs(kcol) + kv_specs(vcol) + [kx, vx]
        args = [sink, rot3, qkv3, rot3, rot3, rot3, qkv3, qkv3, qkv3, ctx3, ctx3]
    else:
        in_specs = [smem, qspec, kx, vx]
        args = [sink, qkv3, ctx3, ctx3]
    out = pl.pallas_call(
        functools.partial(_attn_kernel, window=window, nb=nb),
        grid=(bsz, nb),
        in_specs=in_specs,
        out_specs=pl.BlockSpec((1, BLOCK, nq), lambda b, i: (b, i, 0)),
        out_shape=jax.ShapeDtypeStruct((bsz, seq, nq), BF16),
        compiler_params=_cparams(("arbitrary", "arbitrary")),
        name="attn_window" if window else "attn_ctx",
    )(*args)
    return out.reshape(bsz * seq, nq)


def _outproj_ln_kernel(a_ref, w_ref, h_ref, gate_ref, g_ref, b_ref, o_ref, *, alpha):
    sub = a_ref.shape[0] // ROW_SPLIT
    for k in range(ROW_SPLIT):
        rows = slice(k * sub, (k + 1) * sub)
        y = _dot(a_ref[rows, :], w_ref[...])
        v = alpha * h_ref[rows, :] + gate_ref[0] * y
        o_ref[rows, :] = _layer_norm_rows(v, g_ref[...], b_ref[...])


def _outproj_ln_call(a, w, h, modt, ln_g, ln_b, *, alpha, seq, ctx_row, tm):
    t, k = a.shape
    d = w.shape[1]
    return pl.pallas_call(
        functools.partial(_outproj_ln_kernel, alpha=alpha),
        grid=(t // tm,),
        in_specs=[
            pl.BlockSpec((tm, k), lambda i: (i, 0)),
            pl.BlockSpec((k, d), lambda i: (0, 0)),
            pl.BlockSpec((tm, d), lambda i: (i, 0)),
            pl.BlockSpec((1, 1, d), _mod_index(2, tm, seq, ctx_row)),
            pl.BlockSpec((1, d), lambda i: (0, 0)),
            pl.BlockSpec((1, d), lambda i: (0, 0)),
        ],
        out_specs=pl.BlockSpec((tm, d), lambda i: (i, 0)),
        out_shape=jax.ShapeDtypeStruct((t, d), F32),
        compiler_params=_cparams(("arbitrary",)),
        name="outproj_ln",
    )(a, w, h, modt, ln_g.reshape(1, d), ln_b.reshape(1, d))


def _mlp_ln_kernel(h_ref, sh_ref, sc_ref, gate_ref, w1_ref, w2_ref, g_ref, b_ref, o_ref, u_scr, acc_scr, *, alpha):
    j = pl.program_id(1)

    @pl.when(j == 0)
    def _():
        u_scr[...] = (h_ref[...] * (1.0 + sc_ref[0]) + sh_ref[0]).astype(BF16)
        acc_scr[...] = jnp.zeros_like(acc_scr)

    last = pl.num_programs(1) - 1

    def hidden(rows):
        a = jnp.maximum(_dot(u_scr[rows, :], w1_ref[...]), 0.0)
        return _dot((a * a).astype(BF16), w2_ref[...])

    @pl.when(j < last)
    def _():
        acc_scr[...] += hidden(slice(None))

    @pl.when(j == last)
    def _():
        sub = h_ref.shape[0] // ROW_SPLIT
        for k in range(ROW_SPLIT):
            rows = slice(k * sub, (k + 1) * sub)
            v = alpha * h_ref[rows, :] + gate_ref[0] * (acc_scr[rows, :] + hidden(rows))
            o_ref[rows, :] = _layer_norm_rows(v, g_ref[...], b_ref[...])


def _mlp_ln_call(h, modt, w1, w2, ln_g, ln_b, *, alpha, seq, ctx_row, tm, tf):
    t, d = h.shape
    ff = w1.shape[1]
    return pl.pallas_call(
        functools.partial(_mlp_ln_kernel, alpha=alpha),
        grid=(t // tm, ff // tf),
        in_specs=[
            pl.BlockSpec((tm, d), lambda i, j: (i, 0)),
            pl.BlockSpec((1, 1, d), _mod_index(3, tm, seq, ctx_row)),
            pl.BlockSpec((1, 1, d), _mod_index(4, tm, seq, ctx_row)),
            pl.BlockSpec((1, 1, d), _mod_index(5, tm, seq, ctx_row)),
            pl.BlockSpec((d, tf), lambda i, j: (0, j)),
            pl.BlockSpec((tf, d), lambda i, j: (j, 0)),
            pl.BlockSpec((1, d), lambda i, j: (0, 0)),
            pl.BlockSpec((1, d), lambda i, j: (0, 0)),
        ],
        out_specs=pl.BlockSpec((tm, d), lambda i, j: (i, 0)),
        out_shape=jax.ShapeDtypeStruct((t, d), F32),
        scratch_shapes=[pltpu.VMEM((tm, d), BF16), pltpu.VMEM((tm, d), F32)],
        compiler_params=_cparams(("arbitrary", "arbitrary")),
        name="mlp_ln",
    )(h, modt, modt, modt, w1, w2, ln_g.reshape(1, d), ln_b.reshape(1, d))


CONV_HALO = 8
CONV_ROWS = 256


def _ssm_inproj_kernel(h_ref, sh_ref, sc_ref, w_ref, wdt_ref, dtb_ref, cw_ref, cb_ref, o_ref, dt_ref, u_scr,
                       *, n_dt, n_gate_tiles):
    j = pl.program_id(1)
    seq = h_ref.shape[0]
    tile = o_ref.shape[2]
    sub = min(CONV_ROWS, seq)
    nsub = seq // sub

    @pl.when(j == 0)
    def _():
        u = (h_ref[...] * (1.0 + sc_ref[0]) + sh_ref[0]).astype(BF16)
        u_scr[...] = u
        raw = _dot(u, wdt_ref[...]) + dtb_ref[...]
        sp = jnp.maximum(raw, 0.0) + jnp.log1p(jnp.exp(-jnp.abs(raw)))
        lane = lax.broadcasted_iota(jnp.int32, sp.shape, 1)
        dt_ref[...] = jnp.where(lane < n_dt, sp, 0.0)

    def project(k):
        return _dot(u_scr[k * sub:(k + 1) * sub, :], w_ref[...])

    @pl.when(j < n_gate_tiles)
    def _():
        for k in range(nsub):
            r = project(k)
            o_ref[0, k * sub:(k + 1) * sub, :] = (r * _sigmoid(r)).astype(BF16)

    @pl.when(j >= n_gate_tiles)
    def _():
        w = cw_ref[0]
        bias = cb_ref[0]
        zeros = jnp.zeros((CONV_HALO, tile), F32)
        blocks = [None] * nsub
        total = sub + 2 * CONV_HALO

        def conv(k):
            top = blocks[k - 1][sub - CONV_HALO:] if k > 0 else zeros
            bottom = blocks[k + 1][:CONV_HALO] if k + 1 < nsub else zeros
            xx = jnp.concatenate([top, blocks[k], bottom], axis=0)
            acc = jnp.zeros((sub, tile), F32) + bias
            for t in range(SSM_CONV_W):
                shift = SSM_CONV_W // 2 - t
                rolled = xx if shift == 0 else pltpu.roll(xx, shift % total, 0)
                acc = acc + w[t:t + 1] * rolled[CONV_HALO:CONV_HALO + sub]
            o_ref[0, k * sub:(k + 1) * sub, :] = (acc * _sigmoid(acc)).astype(BF16)

        for k in range(nsub):
            blocks[k] = project(k)
            if k > 0:
                conv(k - 1)
        conv(nsub - 1)


def _ssm_inproj_call(h, modt, w_main, w_dt, dt_bias, conv_w, conv_b, *, n_dt, n_gate_tiles, bsz, seq, ctx_row, tile):
    t, d = h.shape
    ntile = w_main.shape[1] // tile
    return pl.pallas_call(
        functools.partial(_ssm_inproj_kernel, n_dt=n_dt, n_gate_tiles=n_gate_tiles),
        grid=(bsz, ntile),
        in_specs=[
            pl.BlockSpec((seq, d), lambda b, j: (b, 0)),
            pl.BlockSpec((1, 1, d), _mod_index(0, seq, seq, ctx_row)),
            pl.BlockSpec((1, 1, d), _mod_index(1, seq, seq, ctx_row)),
            pl.BlockSpec((d, tile), lambda b, j: (0, j)),
            pl.BlockSpec((d, LANES), lambda b, j: (0, 0)),
            pl.BlockSpec((1, LANES), lambda b, j: (0, 0)),
            pl.BlockSpec((1, 8, tile), lambda b, j: (jnp.maximum(j - n_gate_tiles, 0), 0, 0)),
            pl.BlockSpec((1, 1, tile), lambda b, j: (jnp.maximum(j - n_gate_tiles, 0), 0, 0)),
        ],
        out_specs=[
            pl.BlockSpec((1, seq, tile), lambda b, j: (j, b, 0)),
            pl.BlockSpec((seq, LANES), lambda b, j: (b, 0)),
        ],
        out_shape=[
            jax.ShapeDtypeStruct((ntile, t, tile), BF16),
            jax.ShapeDtypeStruct((t, LANES), F32),
        ],
        scratch_shapes=[pltpu.VMEM((seq, d), BF16)],
        compiler_params=_cparams(("arbitrary", "arbitrary")),
        name="ssm_inproj_conv",
    )(h, modt, modt, w_main, w_dt, dt_bias, conv_w, conv_b)


HEADS_PER_GROUP = 8
DT_LANES_PER_GROUP = 2 * HEADS_PER_GROUP


def _expand_heads(v, lane0, nrows):
    rows = max(nrows, 8)
    if rows != nrows:
        v = jnp.broadcast_to(v, (rows, LANES))
    lane = lax.broadcasted_iota(jnp.int32, (rows, LANES), 1)
    parts = []
    for p in range(HEADS_PER_GROUP // 2):
        idx = lane0 + 2 * p + (lane >> 6)
        parts.append(jnp.take_along_axis(v, idx, axis=1, mode="promise_in_bounds"))
    return jnp.concatenate(parts, axis=1)[:nrows]


def _cumsum_rows(a):
    q = a.shape[0]
    tril = (lax.broadcasted_iota(jnp.int32, (q, q), 0) >= lax.broadcasted_iota(jnp.int32, (q, q), 1))
    tril = jnp.where(tril, 1.0, 0.0).astype(BF16)
    hi = a.astype(BF16)
    r1 = a - hi.astype(F32)
    mid = r1.astype(BF16)
    lo = (r1 - mid.astype(F32)).astype(BF16)
    return _dot(tril, hi) + _dot(tril, mid) + _dot(tril, lo)


def _chunk_terms(dt, a_neg):
    q = dt.shape[0]
    lane = lax.broadcasted_iota(jnp.int32, (q, LANES), 1)
    is_bwd = (lane & HEADS_PER_GROUP) != 0
    a = dt * a_neg
    acs = _cumsum_rows(a)
    tot = acs[q - 1:q, :]
    e = jnp.where(is_bwd, acs - a, acs)
    w_state = dt * jnp.exp(jnp.where(is_bwd, e, tot - acs))
    f_out = jnp.exp(jnp.where(is_bwd, tot - e, acs))
    dec = jnp.exp(tot)
    return e, w_state, f_out, dec


def _state_update(s_scr, g, x, bm, w_state, dec, lane0):
    q = x.shape[0]
    xw = (x.astype(F32) * _expand_heads(w_state, lane0, q)).astype(BF16)
    s_scr[g] = s_scr[g] * _expand_heads(dec, lane0, 1) + _dot_tn(bm, xw)


def _ssd_kernel(alog_ref, d_ref, ng_ref, xc_ref, bc_ref, dtc_ref, xl_ref, bl_ref, cl_ref, zl_ref, dtl_ref,
                y_ref, sf_scr, sb_scr, sbs_scr, *, ncc, ncl):
    s = pl.program_id(1)
    a_neg = -jnp.exp(alog_ref[...])
    q = SSM_CHUNK
    n = SSM_STATE
    gcols = HEADS_PER_GROUP * SSM_HEAD_DIM

    def sweep(s_scr, x_ref, b_ref, dt_ref, dir_off, keep=None):
        _, w_state, _, dec = _chunk_terms(dt_ref[...], a_neg)
        for g in range(SSM_GROUPS):
            if keep is not None:
                sbs_scr[keep, g] = s_scr[g].astype(BF16)
            _state_update(s_scr, g, x_ref[g], b_ref[0, :, g * n:(g + 1) * n], w_state, dec,
                          g * DT_LANES_PER_GROUP + dir_off)

    @pl.when(s == 0)
    def _():
        sf_scr[...] = jnp.zeros_like(sf_scr)
        sb_scr[...] = jnp.zeros_like(sb_scr)

    @pl.when(s < ncc)
    def _():
        sweep(sb_scr, xc_ref, bc_ref, dtc_ref, HEADS_PER_GROUP)

    @pl.when((s >= ncc) & (s < 2 * ncc))
    def _():
        sweep(sf_scr, xc_ref, bc_ref, dtc_ref, 0)

    @pl.when((s >= 2 * ncc) & (s < 2 * ncc + ncl))
    def _():
        sweep(sb_scr, xl_ref, bl_ref, dtl_ref, HEADS_PER_GROUP, keep=ncl - 1 - (s - 2 * ncc))

    @pl.when(s >= 2 * ncc + ncl)
    def _():
        lc = s - 2 * ncc - ncl
        dt = dtl_ref[...]
        e, w_state, f_out, dec = _chunk_terms(dt, a_neg)
        e_t = e.T
        dt_t = dt.T
        li = lax.broadcasted_iota(jnp.int32, (q, q), 0)
        si = lax.broadcasted_iota(jnp.int32, (q, q), 1)
        lower = li >= si
        below = li > si
        above = li < si
        sign = jnp.where(lower, 1.0, -1.0)
        dir_lane = jnp.where(lower, 0, HEADS_PER_GROUP)
        first = lax.broadcasted_iota(jnp.int32, (q, LANES), 1) < SSM_HEAD_DIM
        for g in range(SSM_GROUPS):
            lf = g * DT_LANES_PER_GROUP
            lb = lf + HEADS_PER_GROUP
            x = xl_ref[g]
            bm = bl_ref[0, :, g * n:(g + 1) * n]
            cm = cl_ref[0, :, g * n:(g + 1) * n]
            cb = _dot_nt(cm, bm)
            y_f = _dot(cm, sf_scr[g].astype(BF16))
            y_b = _dot(cm, sbs_scr[lc, g])
            y_parts = []
            for p in range(HEADS_PER_GROUP // 2):
                ms = []
                for r in (2 * p, 2 * p + 1):
                    e_col = jnp.take_along_axis(e, dir_lane + (lf + r), axis=1, mode="promise_in_bounds")
                    e_row = jnp.where(lower, e_t[lf + r:lf + r + 1, :], e_t[lb + r:lb + r + 1, :])
                    dt_f = dt_t[lf + r:lf + r + 1, :]
                    dt_b = dt_t[lb + r:lb + r + 1, :]
                    dsel = jnp.where(below, dt_f, jnp.where(above, dt_b, dt_f + dt_b))
                    ms.append((cb * jnp.exp((e_col - e_row) * sign) * dsel).astype(BF16))
                xp = x[:, p * LANES:(p + 1) * LANES]
                zero = jnp.zeros_like(xp)
                x_bd = jnp.concatenate([jnp.where(first, xp, zero), jnp.where(first, zero, xp)], axis=0)
                y_parts.append(_dot(jnp.concatenate(ms, axis=1), x_bd))
            cols = slice(g * gcols, (g + 1) * gcols)
            y = (jnp.concatenate(y_parts, axis=1)
                 + y_f * _expand_heads(f_out, lf, q) + y_b * _expand_heads(f_out, lb, q)
                 + x.astype(F32) * d_ref[:, cols])
            yy = y * zl_ref[g].astype(F32)
            ms_ = jnp.mean(yy * yy, axis=-1, keepdims=True)
            y_ref[:, cols] = (yy * lax.rsqrt(ms_ + RMS_EPS) * ng_ref[:, cols]).astype(BF16)
            _state_update(sf_scr, g, x, bm, w_state, dec, lf)


def _ssd_call(a_log, d_row, norm_g, zxbc_c, dt_c, zxbc_l, dt_l, *, bsz, n_ctx, seq):
    q = SSM_CHUNK
    ncc = n_ctx // q
    ncl = seq // q
    gcols = zxbc_l.shape[2]
    d_inner = SSM_GROUPS * gcols
    nsteps = 2 * ncc + 2 * ncl
    n_state = SSM_STATE
    xblk = 1
    xt = 2 * SSM_GROUPS

    def cchunk(s):
        return jnp.where(s < ncc, ncc - 1 - s, jnp.where(s < 2 * ncc, s - ncc, ncc - 1))

    def lchunk(s):
        t = s - 2 * ncc
        return jnp.where(t < 0, ncl - 1, jnp.where(t < ncl, ncl - 1 - t, t - ncl))

    def lchunk_fwd(s):
        return jnp.maximum(s - 2 * ncc - ncl, 0)

    in_specs = [
        pl.BlockSpec((1, LANES), lambda b, s: (0, 0)),
        pl.BlockSpec((1, d_inner), lambda b, s: (0, 0)),
        pl.BlockSpec((1, d_inner), lambda b, s: (0, 0)),
        pl.BlockSpec((SSM_GROUPS, q, gcols), lambda b, s: (xblk, b * ncc + cchunk(s), 0)),
        pl.BlockSpec((1, q, SSM_GROUPS * n_state), lambda b, s: (xt, b * ncc + cchunk(s), 0)),
        pl.BlockSpec((q, LANES), lambda b, s: (b * ncc + cchunk(s), 0)),
        pl.BlockSpec((SSM_GROUPS, q, gcols), lambda b, s: (xblk, b * ncl + lchunk(s), 0)),
        pl.BlockSpec((1, q, SSM_GROUPS * n_state), lambda b, s: (xt, b * ncl + lchunk(s), 0)),
        pl.BlockSpec((1, q, SSM_GROUPS * n_state), lambda b, s: (xt + 1, b * ncl + lchunk_fwd(s), 0)),
        pl.BlockSpec((SSM_GROUPS, q, gcols), lambda b, s: (0, b * ncl + lchunk_fwd(s), 0)),
        pl.BlockSpec((q, LANES), lambda b, s: (b * ncl + lchunk(s), 0)),
    ]
    return pl.pallas_call(
        functools.partial(_ssd_kernel, ncc=ncc, ncl=ncl),
        grid=(bsz, nsteps),
        in_specs=in_specs,
        out_specs=pl.BlockSpec((q, d_inner), lambda b, s: (b * ncl + lchunk_fwd(s), 0)),
        out_shape=jax.ShapeDtypeStruct((bsz * seq, d_inner), BF16),
        scratch_shapes=[
            pltpu.VMEM((SSM_GROUPS, n_state, gcols), F32),
            pltpu.VMEM((SSM_GROUPS, n_state, gcols), F32),
            pltpu.VMEM((ncl, SSM_GROUPS, n_state, gcols), BF16),
        ],
        compiler_params=_cparams(("arbitrary", "arbitrary")),
        name="ssd_scan",
    )(a_log, d_row, norm_g, zxbc_c, zxbc_c, dt_c, zxbc_l, zxbc_l, zxbc_l, zxbc_l, dt_l)


def _rope_tables(seq):
    rows = seq // GRID_W
    row = jnp.repeat(jnp.arange(rows), GRID_W).astype(F32)
    col = jnp.tile(jnp.arange(GRID_W), rows).astype(F32)
    inv_freq = ROPE_BASE ** (-jnp.arange(0, ROPE_AXIS_DIM, 2, dtype=F32) / ROPE_AXIS_DIM)
    ang_r = row[:, None] * inv_freq[None, :]
    ang_c = col[:, None] * inv_freq[None, :]
    cos = jnp.concatenate([jnp.cos(ang_r)] * 2 + [jnp.cos(ang_c)] * 2, axis=-1)
    sin = jnp.concatenate([-jnp.sin(ang_r), jnp.sin(ang_r), -jnp.sin(ang_c), jnp.sin(ang_c)], axis=-1)
    reps = LANES // HEAD_DIM
    return jnp.tile(cos, (1, reps)), jnp.tile(sin, (1, reps))


def _dup_heads(w):
    d, n = w.shape
    w = w.reshape(d, n // HEAD_DIM, 1, HEAD_DIM)
    return jnp.broadcast_to(w, (d, n // HEAD_DIM, LANES // HEAD_DIM, HEAD_DIM)).reshape(d, -1)


def _attention_layer(h_lat, h_ctx, modt, w_in, w_out, sink, ln_g, ln_b, *, bsz, seq, n_ctx, ctx_row, alpha, tm):
    nq = ATT_HEADS * HEAD_DIM
    nk = ATT_KV_HEADS * HEAD_DIM
    wq, wk, wv = w_in[:, :nq], w_in[:, nq:nq + nk], w_in[:, nq + nk:]
    wk, wv = _dup_heads(wk), _dup_heads(wv)
    w_cat = jnp.concatenate([wq, wk, wv], axis=1).astype(BF16)
    nrot = nq + wk.shape[1]
    tables = _rope_tables(seq)
    qkv_l, rot_l = _att_inproj_call(h_lat, modt, w_cat, tables, seq=seq, ctx_row=None, nq=nq, nrot=nrot, tm=tm)
    (qkv_c,) = _att_inproj_call(h_ctx, modt, w_cat, None, seq=n_ctx, ctx_row=ctx_row, nq=nq, nrot=nrot,
                                tm=min(tm, h_ctx.shape[0]))
    o_l = _attn_call(sink, qkv_l, rot_l, qkv_c, bsz=bsz, seq=seq, n_ctx=n_ctx, window=True)
    o_c = _attn_call(sink, qkv_c, None, qkv_c, bsz=bsz, seq=n_ctx, n_ctx=n_ctx, window=False)
    w_o = w_out.astype(BF16)
    h_lat = _outproj_ln_call(o_l, w_o, h_lat, modt, ln_g, ln_b, alpha=alpha, seq=seq, ctx_row=None, tm=tm)
    h_ctx = _outproj_ln_call(o_c, w_o, h_ctx, modt, ln_g, ln_b, alpha=alpha, seq=n_ctx, ctx_row=ctx_row,
                             tm=min(tm, h_ctx.shape[0]))
    return h_lat, h_ctx


def _ssm_layer(h_lat, h_ctx, modt, w_in, conv_w, conv_b, dt_bias, a_log, d_skip, norm_g, w_out, ln_g, ln_b,
               *, bsz, seq, n_ctx, ctx_row, alpha, tm):
    d_inner = w_out.shape[0]
    heads = d_inner // SSM_HEAD_DIM
    gcols = d_inner // SSM_GROUPS
    hpg = heads // SSM_GROUPS
    conv_dim = d_inner + 2 * SSM_GROUPS * SSM_STATE
    n_main = d_inner + conv_dim
    assert hpg == HEADS_PER_GROUP and gcols == SSM_GROUPS * SSM_STATE
    w_main = w_in[:, :n_main].astype(BF16)

    def regroup(v):
        lead = v.shape[:-2]
        v = v.reshape(lead + (2, SSM_GROUPS, hpg))
        v = jnp.moveaxis(v, -3, -2).reshape(lead + (SSM_GROUPS, 2 * hpg))
        return v

    w_dt = regroup(w_in[:, n_main:].reshape(-1, 2, heads)).reshape(-1, 2 * heads)
    w_dt = jnp.pad(w_dt, ((0, 0), (0, LANES - 2 * heads))).astype(BF16)
    dtb = jnp.pad(regroup(dt_bias).reshape(1, 2 * heads), ((0, 0), (0, LANES - 2 * heads)))
    a_log_row = jnp.pad(regroup(a_log).reshape(1, 2 * heads), ((0, 0), (0, LANES - 2 * heads)))
    d_row = jnp.repeat(d_skip, SSM_HEAD_DIM).reshape(1, d_inner)
    ng = norm_g.reshape(1, d_inner)
    ntile_conv = conv_dim // gcols
    cw = jnp.pad(conv_w, ((0, 8 - SSM_CONV_W), (0, 0))).reshape(8, ntile_conv, gcols).transpose(1, 0, 2)
    cb = conv_b.reshape(ntile_conv, 1, gcols)

    n_gate = d_inner // gcols
    zxbc_l, dt_l = _ssm_inproj_call(h_lat, modt, w_main, w_dt, dtb, cw, cb, n_dt=2 * heads, n_gate_tiles=n_gate,
                                    bsz=bsz, seq=seq, ctx_row=None, tile=gcols)
    zxbc_c, dt_c = _ssm_inproj_call(h_ctx, modt, w_main, w_dt, dtb, cw, cb, n_dt=2 * heads, n_gate_tiles=n_gate,
                                    bsz=bsz, seq=n_ctx, ctx_row=ctx_row, tile=gcols)
    y = _ssd_call(a_log_row, d_row, ng, zxbc_c, dt_c, zxbc_l, dt_l, bsz=bsz, n_ctx=n_ctx, seq=seq)
    return _outproj_ln_call(y, w_out.astype(BF16), h_lat, modt, ln_g, ln_b, alpha=alpha, seq=seq, ctx_row=None, tm=tm)


def kernel(x, c, ctx, c_ctx, w_mod, b_mod, ln_mix_g, ln_mix_b, ln_ff_g, ln_ff_b, att_w_in, att_w_out, att_sink,
           ssm_w_in, ssm_conv_w, ssm_conv_b, ssm_dt_bias, ssm_a_log, ssm_d, ssm_norm_g, ssm_w_out, ff_w1, ff_w2):
    bsz, seq, d = x.shape
    n_ctx = ctx.shape[1]
    depth = w_mod.shape[0]
    alpha = (2.0 * depth) ** 0.25
    tm = 512
    tm_big = 1024 if seq % 1024 == 0 and (bsz * n_ctx) % 1024 == 0 else tm
    ctx_row = bsz
    rows = -(-(bsz + 1) // MOD_ROWS_ALIGN) * MOD_ROWS_ALIGN
    cc = jnp.concatenate([c, c_ctx[None, :], jnp.zeros((rows - bsz - 1, d), F32)], axis=0)
    mod = _mod_call(cc, w_mod, b_mod)
    h_lat = x.reshape(bsz * seq, d)
    h_ctx = ctx.reshape(bsz * n_ctx, d)
    for i in range(depth):
        last = i == depth - 1
        j = i // N_MIXERS
        modt = mod[i].reshape(rows * N_MOD, 1, d)
        if i % N_MIXERS == 0:
            h_lat, h_ctx_mix = _attention_layer(h_lat, h_ctx, modt, att_w_in[j], att_w_out[j], att_sink[j],
                                                ln_mix_g[i], ln_mix_b[i], bsz=bsz, seq=seq, n_ctx=n_ctx,
                                                ctx_row=ctx_row, alpha=alpha, tm=tm)
        else:
            if not last:
                raise NotImplementedError("context outputs of an SSD layer are only needed when it is not last")
            h_lat = _ssm_layer(h_lat, h_ctx, modt, ssm_w_in[j], ssm_conv_w[j], ssm_conv_b[j], ssm_dt_bias[j],
                               ssm_a_log[j], ssm_d[j], ssm_norm_g[j], ssm_w_out[j], ln_mix_g[i], ln_mix_b[i],
                               bsz=bsz, seq=seq, n_ctx=n_ctx, ctx_row=ctx_row, alpha=alpha, tm=tm)
            h_ctx_mix = None
        w1 = ff_w1[i].astype(BF16)
        w2 = ff_w2[i].astype(BF16)
        h_lat = _mlp_ln_call(h_lat, modt, w1, w2, ln_ff_g[i], ln_ff_b[i], alpha=alpha, seq=seq, ctx_row=None,
                             tm=tm_big, tf=1024)
        if not last:
            h_ctx = _mlp_ln_call(h_ctx_mix, modt, w1, w2, ln_ff_g[i], ln_ff_b[i], alpha=alpha, seq=n_ctx,
                                 ctx_row=ctx_row, tm=min(tm_big, h_ctx_mix.shape[0]), tf=1024)
    return h_lat.reshape(bsz, seq, d)
```

```python
import functools

import jax
import jax.numpy as jnp
from jax import lax
from jax.experimental import pallas as pl
from jax.experimental.pallas import tpu as pltpu

F32 = jnp.float32
BF16 = jnp.bfloat16

GRID_W = 64
N_MIXERS = 2
ATT_HEADS = 16
ATT_KV_HEADS = 4
HEAD_DIM = 64
ATT_GROUP = ATT_HEADS // ATT_KV_HEADS
WINDOW = 128
BLOCK = 128
ROPE_BASE = 10000.0
ROPE_AXIS_DIM = HEAD_DIM // 2

SSM_HEAD_DIM = 64
SSM_GROUPS = 4
SSM_STATE = 128
SSM_CONV_W = 5
SSM_CHUNK = 128

LOG2E = 1.4426950408889634
Q_SCALE = HEAD_DIM ** -0.5 * LOG2E

N_MOD = 6
LN_EPS = 1e-5
RMS_EPS = 1e-5
NEG_INF = -1e30

LANES = 128
MOD_ROWS_ALIGN = 16
ROW_SPLIT = 4
VMEM_LIMIT = 48 * 1024 * 1024


def _cparams(sem):
    return pltpu.CompilerParams(dimension_semantics=sem, vmem_limit_bytes=VMEM_LIMIT)


def _sigmoid(x):
    return 1.0 / (1.0 + jnp.exp(-x))


def _dot(a, b):
    return jnp.dot(a, b, preferred_element_type=F32)


def _dot_nt(a, b):
    return lax.dot_general(a, b, (((1,), (1,)), ((), ())), preferred_element_type=F32)


def _dot_tn(a, b):
    return lax.dot_general(a, b, (((0,), (0,)), ((), ())), preferred_element_type=F32)


def _layer_norm_rows(v, g, b):
    mu = jnp.mean(v, axis=-1, keepdims=True)
    d = v - mu
    var = jnp.mean(d * d, axis=-1, keepdims=True)
    return d * lax.rsqrt(var + LN_EPS) * g + b


def _mod_kernel(c_ref, w_ref, b_ref, o_ref):
    c = c_ref[...]
    s = (c * _sigmoid(c)).astype(BF16)
    o_ref[0] = _dot(s, w_ref[0].astype(BF16)) + b_ref[0]


def _mod_call(cc, w_mod, b_mod):
    depth, d, n = w_mod.shape
    rows = cc.shape[0]
    tn = 1024
    return pl.pallas_call(
        _mod_kernel,
        grid=(depth, n // tn),
        in_specs=[
            pl.BlockSpec((rows, d), lambda l, j: (0, 0)),
            pl.BlockSpec((1, d, tn), lambda l, j: (l, 0, j)),
            pl.BlockSpec((1, 1, tn), lambda l, j: (l, 0, j)),
        ],
        out_specs=pl.BlockSpec((1, rows, tn), lambda l, j: (l, 0, j)),
        out_shape=jax.ShapeDtypeStruct((depth, rows, n), F32),
        compiler_params=_cparams(("arbitrary", "arbitrary")),
        name="adaln_mod",
    )(cc, w_mod, b_mod.reshape(depth, 1, n))


def _mod_index(which, tm, seq, ctx_row):
    if ctx_row is None:
        return lambda i, *_: ((i * tm // seq) * N_MOD + which, 0, 0)
    return lambda i, *_: (ctx_row * N_MOD + which, 0, 0)


def _att_inproj_kernel(*refs, rope, nq):
    if rope:
        x_ref, sh_ref, sc_ref, w_ref, cos_ref, sin_ref, qkv_ref, rot_ref = refs
    else:
        x_ref, sh_ref, sc_ref, w_ref, qkv_ref = refs
    u = (x_ref[...] * (1.0 + sc_ref[0]) + sh_ref[0]).astype(BF16)
    acc = _dot(u, w_ref[...])
    n = acc.shape[1]
    nrot = rot_ref.shape[1] if rope else 0
    if rope:
        cos = cos_ref[...]
        sin = sin_ref[...]
        lane = lax.broadcasted_iota(jnp.int32, cos.shape, 1)
        first_half = (lane & (ROPE_AXIS_DIM - 1)) < (ROPE_AXIS_DIM // 2)
    for cidx in range(n // LANES):
        chunk = acc[:, cidx * LANES:(cidx + 1) * LANES]
        if cidx * LANES < nq:
            chunk = chunk * Q_SCALE
        qkv_ref[:, cidx * LANES:(cidx + 1) * LANES] = chunk.astype(BF16)
        if rope and cidx * LANES < nrot:
            half = ROPE_AXIS_DIM // 2
            swapped = jnp.where(first_half, pltpu.roll(chunk, LANES - half, 1), pltpu.roll(chunk, half, 1))
            rot_ref[:, cidx * LANES:(cidx + 1) * LANES] = (chunk * cos + swapped * sin).astype(BF16)


def _att_inproj_call(h, modt, w, tables, *, seq, ctx_row, nq, nrot, tm):
    t, d = h.shape
    n = w.shape[1]
    rope = tables is not None
    in_specs = [
        pl.BlockSpec((tm, d), lambda i: (i, 0)),
        pl.BlockSpec((1, 1, d), _mod_index(0, tm, seq, ctx_row)),
        pl.BlockSpec((1, 1, d), _mod_index(1, tm, seq, ctx_row)),
        pl.BlockSpec((d, n), lambda i: (0, 0)),
    ]
    args = [h, modt, modt, w]
    out_specs = [pl.BlockSpec((tm, n), lambda i: (i, 0))]
    out_shape = [jax.ShapeDtypeStruct((t, n), BF16)]
    if rope:
        per_seq = seq // tm
        in_specs += [pl.BlockSpec((tm, LANES), lambda i: (i % per_seq, 0))] * 2
        args += list(tables)
        out_specs.append(pl.BlockSpec((tm, nrot), lambda i: (i, 0)))
        out_shape.append(jax.ShapeDtypeStruct((t, nrot), BF16))
    return pl.pallas_call(
        functools.partial(_att_inproj_kernel, rope=rope, nq=nq),
        grid=(t // tm,),
        in_specs=in_specs,
        out_specs=out_specs,
        out_shape=out_shape,
        compiler_params=_cparams(("arbitrary",)),
        name="att_inproj_rope" if rope else "att_inproj",
    )(*args)


def _attn_kernel(*refs, window, nb):
    if window:
        (sink_ref, qr_ref, qp_ref, kp_ref, kc_ref, kn_ref, vp_ref, vc_ref, vn_ref,
         kx_ref, vx_ref, o_ref) = refs
    else:
        sink_ref, qp_ref, kx_ref, vx_ref, o_ref = refs
    i = pl.program_id(1)
    rows = ATT_GROUP * BLOCK
    half_lane = lax.broadcasted_iota(jnp.int32, (BLOCK, LANES), 1) < HEAD_DIM
    ridx = lax.broadcasted_iota(jnp.int32, (rows, 1), 0)
    if window:
        li = lax.broadcasted_iota(jnp.int32, (BLOCK, BLOCK), 0)
        ci = lax.broadcasted_iota(jnp.int32, (BLOCK, BLOCK), 1)
        ok_prev = ci >= li + jnp.where(i > 0, 0, BLOCK)
        ok_next = ci + jnp.where(i < nb - 1, 0, BLOCK) <= li

    def masked(s_blk, ok):
        return jnp.concatenate([jnp.where(ok, s_blk[r * BLOCK:(r + 1) * BLOCK], NEG_INF)
                                for r in range(ATT_GROUP)], axis=0)

    def with_ones(v):
        return jnp.concatenate([v, jnp.ones_like(v)], axis=1)

    def stack_heads(q_ref, g):
        parts = []
        for p in range(ATT_GROUP // 2):
            c0 = (g * (ATT_GROUP // 2) + p) * LANES
            pair = q_ref[0, :, c0:c0 + LANES]
            zero = jnp.zeros_like(pair)
            parts.append(jnp.where(half_lane, pair, zero))
            parts.append(jnp.where(half_lane, zero, pair))
        return jnp.concatenate(parts, axis=0)

    def scores(g):
        gl = slice(g * LANES, (g + 1) * LANES)
        s_ctx = _dot_nt(stack_heads(qp_ref, g), kx_ref[0, :, gl])
        if not window:
            return s_ctx, None
        kw = jnp.concatenate([kp_ref[0, :, gl], kc_ref[0, :, gl], kn_ref[0, :, gl]], axis=0)
        s_win = _dot_nt(stack_heads(qr_ref, g), kw)
        s_win = jnp.concatenate([masked(s_win[:, :BLOCK], ok_prev), s_win[:, BLOCK:2 * BLOCK],
                                 masked(s_win[:, 2 * BLOCK:], ok_next)], axis=1)
        return s_ctx, s_win

    ahead = scores(0)
    for g in range(ATT_KV_HEADS):
        s_ctx, s_win = ahead
        if g + 1 < ATT_KV_HEADS:
            ahead = scores(g + 1)
        gl = slice(g * LANES, (g + 1) * LANES)
        sink_col = jnp.full((rows, 1), sink_ref[g * ATT_GROUP + ATT_GROUP - 1] * LOG2E, F32)
        for r in range(ATT_GROUP - 2, -1, -1):
            sink_col = jnp.where(ridx < (r + 1) * BLOCK, sink_ref[g * ATT_GROUP + r] * LOG2E, sink_col)
        pieces = [s_ctx[:, k * LANES:(k + 1) * LANES] for k in range(s_ctx.shape[1] // LANES)]
        if window:
            pieces += [s_win[:, k * BLOCK:(k + 1) * BLOCK] for k in range(3)]
        mx = pieces[0]
        for piece in pieces[1:]:
            mx = jnp.maximum(mx, piece)
        m = jnp.maximum(jnp.max(mx, axis=-1, keepdims=True), sink_col)
        p_ctx = jnp.exp2(s_ctx - m)
        if window:
            vw = jnp.concatenate([vp_ref[0, :, gl], vc_ref[0, :, gl], vn_ref[0, :, gl]], axis=0)
            acc = (_dot(p_ctx.astype(BF16), with_ones(vx_ref[0, :, gl]))
                   + _dot(jnp.exp2(s_win - m).astype(BF16), with_ones(vw)))
            denom = acc[:, LANES:LANES + 1] + jnp.exp2(sink_col - m)
            o = acc[:, :LANES] * (1.0 / denom)
        else:
            denom = jnp.sum(p_ctx, axis=-1, keepdims=True) + jnp.exp2(sink_col - m)
            o = _dot(p_ctx.astype(BF16), vx_ref[0, :, gl]) * (1.0 / denom)
        for p in range(ATT_GROUP // 2):
            a = o[(2 * p) * BLOCK:(2 * p + 1) * BLOCK]
            b = o[(2 * p + 1) * BLOCK:(2 * p + 2) * BLOCK]
            c0 = (g * (ATT_GROUP // 2) + p) * LANES
            o_ref[0, :, c0:c0 + LANES] = jnp.where(half_lane, a, b).astype(BF16)


def _attn_call(sink, qkv, rot, qkv_ctx, *, bsz, seq, n_ctx, window):
    nq = ATT_HEADS * HEAD_DIM
    nkv = ATT_KV_HEADS * LANES
    nb = seq // BLOCK
    qkv3 = qkv.reshape(bsz, seq, qkv.shape[1])
    ctx3 = qkv_ctx.reshape(bsz, n_ctx, qkv_ctx.shape[1])
    kcol = nq // nkv
    vcol = kcol + 1
    smem = pl.BlockSpec(memory_space=pltpu.SMEM)
    qspec = pl.BlockSpec((1, BLOCK, nq), lambda b, i: (b, i, 0))
    kx = pl.BlockSpec((1, n_ctx, nkv), lambda b, i: (b, 0, kcol))
    vx = pl.BlockSpec((1, n_ctx, nkv), lambda b, i: (b, 0, vcol))
    if window:
        rot3 = rot.reshape(bsz, seq, rot.shape[1])

        def kv_specs(colblk):
            return [
                pl.BlockSpec((1, BLOCK, nkv), lambda b, i: (b, jnp.maximum(i - 1, 0), colblk)),
                pl.BlockSpec((1, BLOCK, nkv), lambda b, i: (b, i, colblk)),
                pl.BlockSpec((1, BLOCK, nkv), lambda b, i: (b, jnp.minimum(i + 1, nb - 1), colblk)),
            ]
        in_specs = [smem, qspec, qspec] + kv_specs(kcol) + kv_specs(vcol) + [kx, vx]
        args = [sink, rot3, qkv3, rot3, rot3, rot3, qkv3, qkv3, qkv3, ctx3, ctx3]
    else:
        in_specs = [smem, qspec, kx, vx]
        args = [sink, qkv3, ctx3, ctx3]
    out = pl.pallas_call(
        functools.partial(_attn_kernel, window=window, nb=nb),
        grid=(bsz, nb),
        in_specs=in_specs,
        out_specs=pl.BlockSpec((1, BLOCK, nq), lambda b, i: (b, i, 0)),
        out_shape=jax.ShapeDtypeStruct((bsz, seq, nq), BF16),
        compiler_params=_cparams(("arbitrary", "arbitrary")),
        name="attn_window" if window else "attn_ctx",
    )(*args)
    return out.reshape(bsz * seq, nq)


def _outproj_ln_kernel(a_ref, w_ref, h_ref, gate_ref, g_ref, b_ref, o_ref, *, alpha):
    sub = a_ref.shape[0] // ROW_SPLIT
    for k in range(ROW_SPLIT):
        rows = slice(k * sub, (k + 1) * sub)
        y = _dot(a_ref[rows, :], w_ref[...])
        v = alpha * h_ref[rows, :] + gate_ref[0] * y
        o_ref[rows, :] = _layer_norm_rows(v, g_ref[...], b_ref[...])


def _outproj_ln_call(a, w, h, modt, ln_g, ln_b, *, alpha, seq, ctx_row, tm):
    t, k = a.shape
    d = w.shape[1]
    return pl.pallas_call(
        functools.partial(_outproj_ln_kernel, alpha=alpha),
        grid=(t // tm,),
        in_specs=[
            pl.BlockSpec((tm, k), lambda i: (i, 0)),
            pl.BlockSpec((k, d), lambda i: (0, 0)),
            pl.BlockSpec((tm, d), lambda i: (i, 0)),
            pl.BlockSpec((1, 1, d), _mod_index(2, tm, seq, ctx_row)),
            pl.BlockSpec((1, d), lambda i: (0, 0)),
            pl.BlockSpec((1, d), lambda i: (0, 0)),
        ],
        out_specs=pl.BlockSpec((tm, d), lambda i: (i, 0)),
        out_shape=jax.ShapeDtypeStruct((t, d), F32),
        compiler_params=_cparams(("arbitrary",)),
        name="outproj_ln",
    )(a, w, h, modt, ln_g.reshape(1, d), ln_b.reshape(1, d))


def _mlp_ln_kernel(h_ref, sh_ref, sc_ref, gate_ref, w1_ref, w2_ref, g_ref, b_ref, o_ref, u_scr, acc_scr, *, alpha):
    j = pl.program_id(1)

    @pl.when(j == 0)
    def _():
        u_scr[...] = (h_ref[...] * (1.0 + sc_ref[0]) + sh_ref[0]).astype(BF16)
        acc_scr[...] = jnp.zeros_like(acc_scr)

    last = pl.num_programs(1) - 1

    def hidden(rows):
        a = jnp.maximum(_dot(u_scr[rows, :], w1_ref[...]), 0.0)
        return _dot((a * a).astype(BF16), w2_ref[...])

    @pl.when(j < last)
    def _():
        acc_scr[...] += hidden(slice(None))

    @pl.when(j == last)
    def _():
        sub = h_ref.shape[0] // ROW_SPLIT
        for k in range(ROW_SPLIT):
            rows = slice(k * sub, (k + 1) * sub)
            v = alpha * h_ref[rows, :] + gate_ref[0] * (acc_scr[rows, :] + hidden(rows))
            o_ref[rows, :] = _layer_norm_rows(v, g_ref[...], b_ref[...])


def _mlp_ln_call(h, modt, w1, w2, ln_g, ln_b, *, alpha, seq, ctx_row, tm, tf):
    t, d = h.shape
    ff = w1.shape[1]
    return pl.pallas_call(
        functools.partial(_mlp_ln_kernel, alpha=alpha),
        grid=(t // tm, ff // tf),
        in_specs=[
            pl.BlockSpec((tm, d), lambda i, j: (i, 0)),
            pl.BlockSpec((1, 1, d), _mod_index(3, tm, seq, ctx_row)),
            pl.BlockSpec((1, 1, d), _mod_index(4, tm, seq, ctx_row)),
            pl.BlockSpec((1, 1, d), _mod_index(5, tm, seq, ctx_row)),
            pl.BlockSpec((d, tf), lambda i, j: (0, j)),
            pl.BlockSpec((tf, d), lambda i, j: (j, 0)),
            pl.BlockSpec((1, d), lambda i, j: (0, 0)),
            pl.BlockSpec((1, d), lambda i, j: (0, 0)),
        ],
        out_specs=pl.BlockSpec((tm, d), lambda i, j: (i, 0)),
        out_shape=jax.ShapeDtypeStruct((t, d), F32),
        scratch_shapes=[pltpu.VMEM((tm, d), BF16), pltpu.VMEM((tm, d), F32)],
        compiler_params=_cparams(("arbitrary", "arbitrary")),
        name="mlp_ln",
    )(h, modt, modt, modt, w1, w2, ln_g.reshape(1, d), ln_b.reshape(1, d))


CONV_HALO = 8
CONV_ROWS = 256


def _ssm_inproj_kernel(h_ref, sh_ref, sc_ref, wg_ref, w_ref, wdt_ref, dtb_ref, cw_ref, cb_ref,
                       gate_ref, o_ref, dt_ref, u_scr, *, n_dt, n_gate_tiles):
    j = pl.program_id(1)
    seq = h_ref.shape[0]
    tile = o_ref.shape[2]
    sub = min(CONV_ROWS, seq)
    nsub = seq // sub

    @pl.when(j == 0)
    def _():
        u = (h_ref[...] * (1.0 + sc_ref[0]) + sh_ref[0]).astype(BF16)
        u_scr[...] = u
        raw = _dot(u, wdt_ref[...]) + dtb_ref[...]
        sp = jnp.maximum(raw, 0.0) + jnp.log1p(jnp.exp(-jnp.abs(raw)))
        lane = lax.broadcasted_iota(jnp.int32, sp.shape, 1)
        dt_ref[...] = jnp.where(lane < n_dt, sp, 0.0)

    def conv_tile(with_gate):
        w = cw_ref[0]
        bias = cb_ref[0]
        zeros = jnp.zeros((CONV_HALO, tile), F32)
        blocks = [None] * nsub
        total = sub + 2 * CONV_HALO

        def conv(k):
            top = blocks[k - 1][sub - CONV_HALO:] if k > 0 else zeros
            bottom = blocks[k + 1][:CONV_HALO] if k + 1 < nsub else zeros
            xx = jnp.concatenate([top, blocks[k], bottom], axis=0)
            acc = jnp.zeros((sub, tile), F32) + bias
            for t in range(SSM_CONV_W):
                shift = SSM_CONV_W // 2 - t
                rolled = xx if shift == 0 else pltpu.roll(xx, shift % total, 0)
                acc = acc + w[t:t + 1] * rolled[CONV_HALO:CONV_HALO + sub]
            o_ref[0, k * sub:(k + 1) * sub, :] = (acc * _sigmoid(acc)).astype(BF16)

        for k in range(nsub):
            rows = slice(k * sub, (k + 1) * sub)
            blocks[k] = _dot(u_scr[rows, :], w_ref[...])
            if with_gate:
                r = _dot(u_scr[rows, :], wg_ref[...])
                gate_ref[0, rows, :] = (r * _sigmoid(r)).astype(BF16)
            if k > 0:
                conv(k - 1)
        conv(nsub - 1)

    @pl.when(j < n_gate_tiles)
    def _():
        conv_tile(True)

    @pl.when(j >= n_gate_tiles)
    def _():
        conv_tile(False)


def _ssm_inproj_call(h, modt, w_main, w_dt, dt_bias, conv_w, conv_b, *, n_dt, n_gate_tiles, bsz, seq, ctx_row, tile):
    t, d = h.shape
    n_conv_tiles = w_main.shape[1] // tile - n_gate_tiles
    assert n_conv_tiles >= n_gate_tiles
    last_gate = n_gate_tiles - 1
    return pl.pallas_call(
        functools.partial(_ssm_inproj_kernel, n_dt=n_dt, n_gate_tiles=n_gate_tiles),
        grid=(bsz, n_conv_tiles),
        in_specs=[
            pl.BlockSpec((seq, d), lambda b, j: (b, 0)),
            pl.BlockSpec((1, 1, d), _mod_index(0, seq, seq, ctx_row)),
            pl.BlockSpec((1, 1, d), _mod_index(1, seq, seq, ctx_row)),
            pl.BlockSpec((d, tile), lambda b, j: (0, jnp.minimum(j, last_gate))),
            pl.BlockSpec((d, tile), lambda b, j: (0, n_gate_tiles + j)),
            pl.BlockSpec((d, LANES), lambda b, j: (0, 0)),
            pl.BlockSpec((1, LANES), lambda b, j: (0, 0)),
            pl.BlockSpec((1, 8, tile), lambda b, j: (j, 0, 0)),
            pl.BlockSpec((1, 1, tile), lambda b, j: (j, 0, 0)),
        ],
        out_specs=[
            pl.BlockSpec((1, seq, tile), lambda b, j: (jnp.minimum(j, last_gate), b, 0)),
            pl.BlockSpec((1, seq, tile), lambda b, j: (j, b, 0)),
            pl.BlockSpec((seq, LANES), lambda b, j: (b, 0)),
        ],
        out_shape=[
            jax.ShapeDtypeStruct((n_gate_tiles, t, tile), BF16),
            jax.ShapeDtypeStruct((n_conv_tiles, t, tile), BF16),
            jax.ShapeDtypeStruct((t, LANES), F32),
        ],
        scratch_shapes=[pltpu.VMEM((seq, d), BF16)],
        compiler_params=_cparams(("arbitrary", "arbitrary")),
        name="ssm_inproj_conv",
    )(h, modt, modt, w_main, w_main, w_dt, dt_bias, conv_w, conv_b)


HEADS_PER_GROUP = 8
DT_LANES_PER_GROUP = 2 * HEADS_PER_GROUP


def _expand_heads(v, lane0, nrows):
    rows = max(nrows, 8)
    if rows != nrows:
        v = jnp.broadcast_to(v, (rows, LANES))
    lane = lax.broadcasted_iota(jnp.int32, (rows, LANES), 1)
    parts = []
    for p in range(HEADS_PER_GROUP // 2):
        idx = lane0 + 2 * p + (lane >> 6)
        parts.append(jnp.take_along_axis(v, idx, axis=1, mode="promise_in_bounds"))
    return jnp.concatenate(parts, axis=1)[:nrows]


def _cumsum_rows(a):
    q = a.shape[0]
    tril = (lax.broadcasted_iota(jnp.int32, (q, q), 0) >= lax.broadcasted_iota(jnp.int32, (q, q), 1))
    tril = jnp.where(tril, 1.0, 0.0).astype(BF16)
    hi = a.astype(BF16)
    r1 = a - hi.astype(F32)
    mid = r1.astype(BF16)
    lo = (r1 - mid.astype(F32)).astype(BF16)
    return _dot(tril, hi) + _dot(tril, mid) + _dot(tril, lo)


def _chunk_terms(dt, a_neg):
    q = dt.shape[0]
    lane = lax.broadcasted_iota(jnp.int32, (q, LANES), 1)
    is_bwd = (lane & HEADS_PER_GROUP) != 0
    a = dt * a_neg
    acs = _cumsum_rows(a)
    tot = acs[q - 1:q, :]
    e = jnp.where(is_bwd, acs - a, acs)
    w_state = dt * jnp.exp(jnp.where(is_bwd, e, tot - acs))
    f_out = jnp.exp(jnp.where(is_bwd, tot - e, acs))
    dec = jnp.exp(tot)
    return e, w_state, f_out, dec


def _state_update(s_scr, idx, x, bm, w_state, dec, lane0):
    q = x.shape[0]
    xw = (x.astype(F32) * _expand_heads(w_state, lane0, q)).astype(BF16)
    s_scr[idx] = s_scr[idx] * _expand_heads(dec, lane0, 1) + _dot_tn(bm, xw)


def _ssd_kernel(alog_ref, d_ref, ng_ref, xc_ref, dtc_ref, xb_ref, dtb_ref, xf_ref, zf_ref, dtf_ref,
                y_ref, sf_scr, sb_scr, sbs_scr, *, nb, ncc, ncl):
    o = pl.program_id(0)
    s = pl.program_id(1)
    par = o % 2
    prev = 1 - par
    entering = o < nb
    emitting = o >= 1
    lat = s >= 2 * ncc
    t = s - 2 * ncc
    a_neg = -jnp.exp(alog_ref[...])
    q = SSM_CHUNK
    n = SSM_STATE
    gcols = HEADS_PER_GROUP * SSM_HEAD_DIM

    def sweep_steps(s_scr, slot, x_ref, dt_ref, dir_off, keep=None):
        _, w_state, _, dec = _chunk_terms(dt_ref[...], a_neg)

        def one(g):
            idx = g if slot is None else (slot, g)
            if keep is not None:
                sbs_scr[par, keep, g] = s_scr[idx].astype(BF16)
            _state_update(s_scr, idx, x_ref[g], x_ref[SSM_GROUPS, :, g * n:(g + 1) * n], w_state, dec,
                          g * DT_LANES_PER_GROUP + dir_off)
        return [functools.partial(one, g) for g in range(SSM_GROUPS)]

    def sweep(*args, **kwargs):
        for step in sweep_steps(*args, **kwargs):
            step()

    @pl.when((s == 0) & entering)
    def _():
        sf_scr[par] = jnp.zeros(sf_scr.shape[1:], F32)
        sb_scr[...] = jnp.zeros_like(sb_scr)

    @pl.when((s < ncc) & entering)
    def _():
        sweep(sb_scr, None, xc_ref, dtc_ref, HEADS_PER_GROUP)

    @pl.when((s >= ncc) & (s < 2 * ncc) & entering)
    def _():
        sweep(sf_scr, par, xc_ref, dtc_ref, 0)

    def backward_steps():
        return sweep_steps(sb_scr, None, xb_ref, dtb_ref, HEADS_PER_GROUP, keep=ncl - 1 - t)

    def forward(between=None):
        dt = dtf_ref[...]
        e, w_state, f_out, dec = _chunk_terms(dt, a_neg)
        e_t = e.T
        dt_t = dt.T
        li = lax.broadcasted_iota(jnp.int32, (q, q), 0)
        si = lax.broadcasted_iota(jnp.int32, (q, q), 1)
        lower = li >= si
        below = li > si
        above = li < si
        sign = jnp.where(lower, 1.0, -1.0)
        dir_lane = jnp.where(lower, 0, HEADS_PER_GROUP)
        first = lax.broadcasted_iota(jnp.int32, (q, LANES), 1) < SSM_HEAD_DIM
        for g in range(SSM_GROUPS):
            lf = g * DT_LANES_PER_GROUP
            lb = lf + HEADS_PER_GROUP
            x = xf_ref[g]
            bm = xf_ref[SSM_GROUPS, :, g * n:(g + 1) * n]
            cm = xf_ref[SSM_GROUPS + 1, :, g * n:(g + 1) * n]
            cb = _dot_nt(cm, bm)
            y_f = _dot(cm, sf_scr[prev, g].astype(BF16))
            y_b = _dot(cm, sbs_scr[prev, t, g])
            y_parts = []
            for p in range(HEADS_PER_GROUP // 2):
                ms = []
                for r in (2 * p, 2 * p + 1):
                    e_col = jnp.take_along_axis(e, dir_lane + (lf + r), axis=1, mode="promise_in_bounds")
                    e_row = jnp.where(lower, e_t[lf + r:lf + r + 1, :], e_t[lb + r:lb + r + 1, :])
                    dt_f = dt_t[lf + r:lf + r + 1, :]
                    dt_b = dt_t[lb + r:lb + r + 1, :]
                    dsel = jnp.where(below, dt_f, jnp.where(above, dt_b, dt_f + dt_b))
                    ms.append((cb * jnp.exp((e_col - e_row) * sign) * dsel).astype(BF16))
                xp = x[:, p * LANES:(p + 1) * LANES]
                zero = jnp.zeros_like(xp)
                x_bd = jnp.concatenate([jnp.where(first, xp, zero), jnp.where(first, zero, xp)], axis=0)
                y_parts.append(_dot(jnp.concatenate(ms, axis=1), x_bd))
            cols = slice(g * gcols, (g + 1) * gcols)
            y = (jnp.concatenate(y_parts, axis=1)
                 + y_f * _expand_heads(f_out, lf, q) + y_b * _expand_heads(f_out, lb, q)
                 + x.astype(F32) * d_ref[:, cols])
            yy = y * zf_ref[g].astype(F32)
            ms_ = jnp.mean(yy * yy, axis=-1, keepdims=True)
            y_ref[:, cols] = (yy * lax.rsqrt(ms_ + RMS_EPS) * ng_ref[:, cols]).astype(BF16)
            _state_update(sf_scr, (prev, g), x, bm, w_state, dec, lf)
            if between is not None:
                between[g]()

    @pl.when(lat & entering & emitting)
    def _():
        forward(between=backward_steps())

    @pl.when(lat & jnp.logical_not(emitting))
    def _():
        for step in backward_steps():
            step()

    @pl.when(lat & jnp.logical_not(entering))
    def _():
        forward()


def _ssd_call(a_log, d_row, norm_g, xbc_c, dt_c, gate_l, xbc_l, dt_l, *, bsz, n_ctx, seq):
    q = SSM_CHUNK
    ncc = n_ctx // q
    ncl = seq // q
    gcols = xbc_l.shape[2]
    d_inner = SSM_GROUPS * gcols
    nsteps = 2 * ncc + ncl
    n_state = SSM_STATE

    def entering(o):
        return jnp.minimum(o, bsz - 1)

    def emitting(o):
        return jnp.maximum(o - 1, 0)

    def cchunk(s):
        return jnp.where(s < ncc, ncc - 1 - s, jnp.where(s < 2 * ncc, s - ncc, ncc - 1))

    def bchunk(s):
        return ncl - 1 - jnp.maximum(s - 2 * ncc, 0)

    def fchunk(s):
        return jnp.maximum(s - 2 * ncc, 0)

    def crow(o, s):
        return entering(o) * ncc + cchunk(s)

    def brow(o, s):
        return entering(o) * ncl + bchunk(s)

    def frow(o, s):
        return emitting(o) * ncl + fchunk(s)

    def orow(o, s):
        return jnp.where(o == 0, 0, frow(o, s))

    assert SSM_GROUPS * n_state == gcols
    cspec = (xbc_l.shape[0], q, gcols)
    in_specs = [
        pl.BlockSpec((1, LANES), lambda o, s: (0, 0)),
        pl.BlockSpec((1, d_inner), lambda o, s: (0, 0)),
        pl.BlockSpec((1, d_inner), lambda o, s: (0, 0)),
        pl.BlockSpec(cspec, lambda o, s: (0, crow(o, s), 0)),
        pl.BlockSpec((q, LANES), lambda o, s: (crow(o, s), 0)),
        pl.BlockSpec(cspec, lambda o, s: (0, brow(o, s), 0)),
        pl.BlockSpec((q, LANES), lambda o, s: (brow(o, s), 0)),
        pl.BlockSpec(cspec, lambda o, s: (0, frow(o, s), 0)),
        pl.BlockSpec((SSM_GROUPS, q, gcols), lambda o, s: (0, frow(o, s), 0)),
        pl.BlockSpec((q, LANES), lambda o, s: (frow(o, s), 0)),
    ]
    return pl.pallas_call(
        functools.partial(_ssd_kernel, nb=bsz, ncc=ncc, ncl=ncl),
        grid=(bsz + 1, nsteps),
        in_specs=in_specs,
        out_specs=pl.BlockSpec((q, d_inner), lambda o, s: (orow(o, s), 0)),
        out_shape=jax.ShapeDtypeStruct((bsz * seq, d_inner), BF16),
        scratch_shapes=[
            pltpu.VMEM((2, SSM_GROUPS, n_state, gcols), F32),
            pltpu.VMEM((SSM_GROUPS, n_state, gcols), F32),
            pltpu.VMEM((2, ncl, SSM_GROUPS, n_state, gcols), BF16),
        ],
        compiler_params=_cparams(("arbitrary", "arbitrary")),
        name="ssd_scan",
    )(a_log, d_row, norm_g, xbc_c, dt_c, xbc_l, dt_l, xbc_l, gate_l, dt_l)


def _rope_tables(seq):
    rows = seq // GRID_W
    row = jnp.repeat(jnp.arange(rows), GRID_W).astype(F32)
    col = jnp.tile(jnp.arange(GRID_W), rows).astype(F32)
    inv_freq = ROPE_BASE ** (-jnp.arange(0, ROPE_AXIS_DIM, 2, dtype=F32) / ROPE_AXIS_DIM)
    ang_r = row[:, None] * inv_freq[None, :]
    ang_c = col[:, None] * inv_freq[None, :]
    cos = jnp.concatenate([jnp.cos(ang_r)] * 2 + [jnp.cos(ang_c)] * 2, axis=-1)
    sin = jnp.concatenate([-jnp.sin(ang_r), jnp.sin(ang_r), -jnp.sin(ang_c), jnp.sin(ang_c)], axis=-1)
    reps = LANES // HEAD_DIM
    return jnp.tile(cos, (1, reps)), jnp.tile(sin, (1, reps))


def _dup_heads(w):
    d, n = w.shape
    w = w.reshape(d, n // HEAD_DIM, 1, HEAD_DIM)
    return jnp.broadcast_to(w, (d, n // HEAD_DIM, LANES // HEAD_DIM, HEAD_DIM)).reshape(d, -1)


def _attention_layer(h_lat, h_ctx, modt, w_in, w_out, sink, ln_g, ln_b, *, bsz, seq, n_ctx, ctx_row, alpha, tm):
    nq = ATT_HEADS * HEAD_DIM
    nk = ATT_KV_HEADS * HEAD_DIM
    wq, wk, wv = w_in[:, :nq], w_in[:, nq:nq + nk], w_in[:, nq + nk:]
    wk, wv = _dup_heads(wk), _dup_heads(wv)
    w_cat = jnp.concatenate([wq, wk, wv], axis=1).astype(BF16)
    nrot = nq + wk.shape[1]
    tables = _rope_tables(seq)
    qkv_l, rot_l = _att_inproj_call(h_lat, modt, w_cat, tables, seq=seq, ctx_row=None, nq=nq, nrot=nrot, tm=tm)
    (qkv_c,) = _att_inproj_call(h_ctx, modt, w_cat, None, seq=n_ctx, ctx_row=ctx_row, nq=nq, nrot=nrot,
                                tm=min(tm, h_ctx.shape[0]))
    o_l = _attn_call(sink, qkv_l, rot_l, qkv_c, bsz=bsz, seq=seq, n_ctx=n_ctx, window=True)
    o_c = _attn_call(sink, qkv_c, None, qkv_c, bsz=bsz, seq=n_ctx, n_ctx=n_ctx, window=False)
    w_o = w_out.astype(BF16)
    h_lat = _outproj_ln_call(o_l, w_o, h_lat, modt, ln_g, ln_b, alpha=alpha, seq=seq, ctx_row=None, tm=tm)
    h_ctx = _outproj_ln_call(o_c, w_o, h_ctx, modt, ln_g, ln_b, alpha=alpha, seq=n_ctx, ctx_row=ctx_row,
                             tm=min(tm, h_ctx.shape[0]))
    return h_lat, h_ctx


def _ssm_layer(h_lat, h_ctx, modt, w_in, conv_w, conv_b, dt_bias, a_log, d_skip, norm_g, w_out, ln_g, ln_b,
               *, bsz, seq, n_ctx, ctx_row, alpha, tm):
    d_inner = w_out.shape[0]
    heads = d_inner // SSM_HEAD_DIM
    gcols = d_inner // SSM_GROUPS
    hpg = heads // SSM_GROUPS
    conv_dim = d_inner + 2 * SSM_GROUPS * SSM_STATE
    n_main = d_inner + conv_dim
    assert hpg == HEADS_PER_GROUP and gcols == SSM_GROUPS * SSM_STATE
    w_main = w_in[:, :n_main].astype(BF16)

    def regroup(v):
        lead = v.shape[:-2]
        v = v.reshape(lead + (2, SSM_GROUPS, hpg))
        v = jnp.moveaxis(v, -3, -2).reshape(lead + (SSM_GROUPS, 2 * hpg))
        return v

    w_dt = regroup(w_in[:, n_main:].reshape(-1, 2, heads)).reshape(-1, 2 * heads)
    w_dt = jnp.pad(w_dt, ((0, 0), (0, LANES - 2 * heads))).astype(BF16)
    dtb = jnp.pad(regroup(dt_bias).reshape(1, 2 * heads), ((0, 0), (0, LANES - 2 * heads)))
    a_log_row = jnp.pad(regroup(a_log).reshape(1, 2 * heads), ((0, 0), (0, LANES - 2 * heads)))
    d_row = jnp.repeat(d_skip, SSM_HEAD_DIM).reshape(1, d_inner)
    ng = norm_g.reshape(1, d_inner)
    ntile_conv = conv_dim // gcols
    cw = jnp.pad(conv_w, ((0, 8 - SSM_CONV_W), (0, 0))).reshape(8, ntile_conv, gcols).transpose(1, 0, 2)
    cb = conv_b.reshape(ntile_conv, 1, gcols)

    n_gate = d_inner // gcols
    gate_l, xbc_l, dt_l = _ssm_inproj_call(h_lat, modt, w_main, w_dt, dtb, cw, cb, n_dt=2 * heads,
                                           n_gate_tiles=n_gate, bsz=bsz, seq=seq, ctx_row=None, tile=gcols)
    _, xbc_c, dt_c = _ssm_inproj_call(h_ctx, modt, w_main, w_dt, dtb, cw, cb, n_dt=2 * heads,
                                      n_gate_tiles=n_gate, bsz=bsz, seq=n_ctx, ctx_row=ctx_row, tile=gcols)
    y = _ssd_call(a_log_row, d_row, ng, xbc_c, dt_c, gate_l, xbc_l, dt_l, bsz=bsz, n_ctx=n_ctx, seq=seq)
    return _outproj_ln_call(y, w_out.astype(BF16), h_lat, modt, ln_g, ln_b, alpha=alpha, seq=seq, ctx_row=None, tm=tm)


def kernel(x, c, ctx, c_ctx, w_mod, b_mod, ln_mix_g, ln_mix_b, ln_ff_g, ln_ff_b, att_w_in, att_w_out, att_sink,
           ssm_w_in, ssm_conv_w, ssm_conv_b, ssm_dt_bias, ssm_a_log, ssm_d, ssm_norm_g, ssm_w_out, ff_w1, ff_w2):
    bsz, seq, d = x.shape
    n_ctx = ctx.shape[1]
    depth = w_mod.shape[0]
    alpha = (2.0 * depth) ** 0.25
    tm = 512
    tm_big = 1024 if seq % 1024 == 0 and (bsz * n_ctx) % 1024 == 0 else tm
    ctx_row = bsz
    rows = -(-(bsz + 1) // MOD_ROWS_ALIGN) * MOD_ROWS_ALIGN
    cc = jnp.concatenate([c, c_ctx[None, :], jnp.zeros((rows - bsz - 1, d), F32)], axis=0)
    mod = _mod_call(cc, w_mod, b_mod)
    h_lat = x.reshape(bsz * seq, d)
    h_ctx = ctx.reshape(bsz * n_ctx, d)
    for i in range(depth):
        last = i == depth - 1
        j = i // N_MIXERS
        modt = mod[i].reshape(rows * N_MOD, 1, d)
        if i % N_MIXERS == 0:
            h_lat, h_ctx_mix = _attention_layer(h_lat, h_ctx, modt, att_w_in[j], att_w_out[j], att_sink[j],
                                                ln_mix_g[i], ln_mix_b[i], bsz=bsz, seq=seq, n_ctx=n_ctx,
                                                ctx_row=ctx_row, alpha=alpha, tm=tm)
        else:
            if not last:
                raise NotImplementedError("context outputs of an SSD layer are only needed when it is not last")
            h_lat = _ssm_layer(h_lat, h_ctx, modt, ssm_w_in[j], ssm_conv_w[j], ssm_conv_b[j], ssm_dt_bias[j],
                               ssm_a_log[j], ssm_d[j], ssm_norm_g[j], ssm_w_out[j], ln_mix_g[i], ln_mix_b[i],
                               bsz=bsz, seq=seq, n_ctx=n_ctx, ctx_row=ctx_row, alpha=alpha, tm=tm)
            h_ctx_mix = None
        w1 = ff_w1[i].astype(BF16)
        w2 = ff_w2[i].astype(BF16)
        h_lat = _mlp_ln_call(h_lat, modt, w1, w2, ln_ff_g[i], ln_ff_b[i], alpha=alpha, seq=seq, ctx_row=None,
                             tm=tm_big, tf=1024)
        if not last:
            h_ctx = _mlp_ln_call(h_ctx_mix, modt, w1, w2, ln_ff_g[i], ln_ff_b[i], alpha=alpha, seq=n_ctx,
                                 ctx_row=ctx_row, tm=min(tm_big, h_ctx_mix.shape[0]), tf=1024)
    return h_lat.reshape(bsz, seq, d)
```

```python
import functools

import jax
import jax.numpy as jnp
from jax import lax
from jax.experimental import pallas as pl
from jax.experimental.pallas import tpu as pltpu

F32 = jnp.float32
BF16 = jnp.bfloat16

GRID_W = 64
N_MIXERS = 2
ATT_HEADS = 16
ATT_KV_HEADS = 4
HEAD_DIM = 64
ATT_GROUP = ATT_HEADS // ATT_KV_HEADS
WINDOW = 128
BLOCK = 128
ROPE_BASE = 10000.0
ROPE_AXIS_DIM = HEAD_DIM // 2

SSM_HEAD_DIM = 64
SSM_GROUPS = 4
SSM_STATE = 128
SSM_CONV_W = 5
SSM_CHUNK = 128

LOG2E = 1.4426950408889634
Q_SCALE = HEAD_DIM ** -0.5 * LOG2E

N_MOD = 6
LN_EPS = 1e-5
RMS_EPS = 1e-5
NEG_INF = -1e30

LANES = 128
MOD_ROWS_ALIGN = 16
ROW_SPLIT = 4
VMEM_LIMIT = 48 * 1024 * 1024


def _cparams(sem):
    return pltpu.CompilerParams(dimension_semantics=sem, vmem_limit_bytes=VMEM_LIMIT)


def _sigmoid(x):
    return 1.0 / (1.0 + jnp.exp(-x))


def _dot(a, b):
    return jnp.dot(a, b, preferred_element_type=F32)


def _dot_nt(a, b):
    return lax.dot_general(a, b, (((1,), (1,)), ((), ())), preferred_element_type=F32)


def _dot_tn(a, b):
    return lax.dot_general(a, b, (((0,), (0,)), ((), ())), preferred_element_type=F32)


def _layer_norm_rows(v, g, b):
    mu = jnp.mean(v, axis=-1, keepdims=True)
    d = v - mu
    var = jnp.mean(d * d, axis=-1, keepdims=True)
    return d * lax.rsqrt(var + LN_EPS) * g + b


def _mod_kernel(c_ref, w_ref, b_ref, o_ref):
    c = c_ref[...]
    s = (c * _sigmoid(c)).astype(BF16)
    o_ref[0] = _dot(s, w_ref[0].astype(BF16)) + b_ref[0]


def _mod_call(cc, w_mod, b_mod):
    depth, d, n = w_mod.shape
    rows = cc.shape[0]
    tn = 1024
    return pl.pallas_call(
        _mod_kernel,
        grid=(depth, n // tn),
        in_specs=[
            pl.BlockSpec((rows, d), lambda l, j: (0, 0)),
            pl.BlockSpec((1, d, tn), lambda l, j: (l, 0, j)),
            pl.BlockSpec((1, 1, tn), lambda l, j: (l, 0, j)),
        ],
        out_specs=pl.BlockSpec((1, rows, tn), lambda l, j: (l, 0, j)),
        out_shape=jax.ShapeDtypeStruct((depth, rows, n), F32),
        compiler_params=_cparams(("arbitrary", "arbitrary")),
        name="adaln_mod",
    )(cc, w_mod, b_mod.reshape(depth, 1, n))


def _mod_index(which, tm, seq, ctx_row):
    if ctx_row is None:
        return lambda i, *_: ((i * tm // seq) * N_MOD + which, 0, 0)
    return lambda i, *_: (ctx_row * N_MOD + which, 0, 0)


def _att_inproj_kernel(*refs, rope, nq):
    if rope:
        x_ref, sh_ref, sc_ref, w_ref, cos_ref, sin_ref, qkv_ref, rot_ref = refs
    else:
        x_ref, sh_ref, sc_ref, w_ref, qkv_ref = refs
    u = (x_ref[...] * (1.0 + sc_ref[0]) + sh_ref[0]).astype(BF16)
    acc = _dot(u, w_ref[...])
    n = acc.shape[1]
    nrot = rot_ref.shape[1] if rope else 0
    if rope:
        cos = cos_ref[...]
        sin = sin_ref[...]
        lane = lax.broadcasted_iota(jnp.int32, cos.shape, 1)
        first_half = (lane & (ROPE_AXIS_DIM - 1)) < (ROPE_AXIS_DIM // 2)
    for cidx in range(n // LANES):
        chunk = acc[:, cidx * LANES:(cidx + 1) * LANES]
        if cidx * LANES < nq:
            chunk = chunk * Q_SCALE
        qkv_ref[:, cidx * LANES:(cidx + 1) * LANES] = chunk.astype(BF16)
        if rope and cidx * LANES < nrot:
            half = ROPE_AXIS_DIM // 2
            swapped = jnp.where(first_half, pltpu.roll(chunk, LANES - half, 1), pltpu.roll(chunk, half, 1))
            rot_ref[:, cidx * LANES:(cidx + 1) * LANES] = (chunk * cos + swapped * sin).astype(BF16)


def _att_inproj_call(h, modt, w, tables, *, seq, ctx_row, nq, nrot, tm):
    t, d = h.shape
    n = w.shape[1]
    rope = tables is not None
    in_specs = [
        pl.BlockSpec((tm, d), lambda i: (i, 0)),
        pl.BlockSpec((1, 1, d), _mod_index(0, tm, seq, ctx_row)),
        pl.BlockSpec((1, 1, d), _mod_index(1, tm, seq, ctx_row)),
        pl.BlockSpec((d, n), lambda i: (0, 0)),
    ]
    args = [h, modt, modt, w]
    out_specs = [pl.BlockSpec((tm, n), lambda i: (i, 0))]
    out_shape = [jax.ShapeDtypeStruct((t, n), BF16)]
    if rope:
        per_seq = seq // tm
        in_specs += [pl.BlockSpec((tm, LANES), lambda i: (i % per_seq, 0))] * 2
        args += list(tables)
        out_specs.append(pl.BlockSpec((tm, nrot), lambda i: (i, 0)))
        out_shape.append(jax.ShapeDtypeStruct((t, nrot), BF16))
    return pl.pallas_call(
        functools.partial(_att_inproj_kernel, rope=rope, nq=nq),
        grid=(t // tm,),
        in_specs=in_specs,
        out_specs=out_specs,
        out_shape=out_shape,
        compiler_params=_cparams(("arbitrary",)),
        name="att_inproj_rope" if rope else "att_inproj",
    )(*args)


def _attn_kernel(*refs, window, nb):
    if window:
        (sink_ref, qr_ref, qp_ref, kp_ref, kc_ref, kn_ref, vp_ref, vc_ref, vn_ref,
         kx_ref, vx_ref, o_ref) = refs
    else:
        sink_ref, qp_ref, kx_ref, vx_ref, o_ref = refs
    i = pl.program_id(1)
    rows = ATT_GROUP * BLOCK
    half_lane = lax.broadcasted_iota(jnp.int32, (BLOCK, LANES), 1) < HEAD_DIM
    ridx = lax.broadcasted_iota(jnp.int32, (rows, 1), 0)
    if window:
        li = lax.broadcasted_iota(jnp.int32, (BLOCK, BLOCK), 0)
        ci = lax.broadcasted_iota(jnp.int32, (BLOCK, BLOCK), 1)
        ok_prev = ci >= li + jnp.where(i > 0, 0, BLOCK)
        ok_next = ci + jnp.where(i < nb - 1, 0, BLOCK) <= li

    def masked(s_blk, ok):
        return jnp.concatenate([jnp.where(ok, s_blk[r * BLOCK:(r + 1) * BLOCK], NEG_INF)
                                for r in range(ATT_GROUP)], axis=0)

    def with_ones(v):
        return jnp.concatenate([v, jnp.ones_like(v)], axis=1)

    def stack_heads(q_ref, g):
        parts = []
        for p in range(ATT_GROUP // 2):
            c0 = (g * (ATT_GROUP // 2) + p) * LANES
            pair = q_ref[0, :, c0:c0 + LANES]
            zero = jnp.zeros_like(pair)
            parts.append(jnp.where(half_lane, pair, zero))
            parts.append(jnp.where(half_lane, zero, pair))
        return jnp.concatenate(parts, axis=0)

    def scores(g):
        gl = slice(g * LANES, (g + 1) * LANES)
        s_ctx = _dot_nt(stack_heads(qp_ref, g), kx_ref[0, :, gl])
        if not window:
            return s_ctx, None
        kw = jnp.concatenate([kp_ref[0, :, gl], kc_ref[0, :, gl], kn_ref[0, :, gl]], axis=0)
        s_win = _dot_nt(stack_heads(qr_ref, g), kw)
        s_win = jnp.concatenate([masked(s_win[:, :BLOCK], ok_prev), s_win[:, BLOCK:2 * BLOCK],
                                 masked(s_win[:, 2 * BLOCK:], ok_next)], axis=1)
        return s_ctx, s_win

    ahead = scores(0)
    for g in range(ATT_KV_HEADS):
        s_ctx, s_win = ahead
        if g + 1 < ATT_KV_HEADS:
            ahead = scores(g + 1)
        gl = slice(g * LANES, (g + 1) * LANES)
        sink_col = jnp.full((rows, 1), sink_ref[g * ATT_GROUP + ATT_GROUP - 1] * LOG2E, F32)
        for r in range(ATT_GROUP - 2, -1, -1):
            sink_col = jnp.where(ridx < (r + 1) * BLOCK, sink_ref[g * ATT_GROUP + r] * LOG2E, sink_col)
        pieces = [s_ctx[:, k * LANES:(k + 1) * LANES] for k in range(s_ctx.shape[1] // LANES)]
        if window:
            pieces += [s_win[:, k * BLOCK:(k + 1) * BLOCK] for k in range(3)]
        mx = pieces[0]
        for piece in pieces[1:]:
            mx = jnp.maximum(mx, piece)
        m = jnp.maximum(jnp.max(mx, axis=-1, keepdims=True), sink_col)
        p_ctx = jnp.exp2(s_ctx - m)
        if window:
            vw = jnp.concatenate([vp_ref[0, :, gl], vc_ref[0, :, gl], vn_ref[0, :, gl]], axis=0)
            acc = (_dot(p_ctx.astype(BF16), with_ones(vx_ref[0, :, gl]))
                   + _dot(jnp.exp2(s_win - m).astype(BF16), with_ones(vw)))
            denom = acc[:, LANES:LANES + 1] + jnp.exp2(sink_col - m)
            o = acc[:, :LANES] * (1.0 / denom)
        else:
            denom = jnp.sum(p_ctx, axis=-1, keepdims=True) + jnp.exp2(sink_col - m)
            o = _dot(p_ctx.astype(BF16), vx_ref[0, :, gl]) * (1.0 / denom)
        for p in range(ATT_GROUP // 2):
            a = o[(2 * p) * BLOCK:(2 * p + 1) * BLOCK]
            b = o[(2 * p + 1) * BLOCK:(2 * p + 2) * BLOCK]
            c0 = (g * (ATT_GROUP // 2) + p) * LANES
            o_ref[0, :, c0:c0 + LANES] = jnp.where(half_lane, a, b).astype(BF16)


def _attn_call(sink, qkv, rot, qkv_ctx, *, bsz, seq, n_ctx, window):
    nq = ATT_HEADS * HEAD_DIM
    nkv = ATT_KV_HEADS * LANES
    nb = seq // BLOCK
    qkv3 = qkv.reshape(bsz, seq, qkv.shape[1])
    ctx3 = qkv_ctx.reshape(bsz, n_ctx, qkv_ctx.shape[1])
    kcol = nq // nkv
    vcol = kcol + 1
    smem = pl.BlockSpec(memory_space=pltpu.SMEM)
    qspec = pl.BlockSpec((1, BLOCK, nq), lambda b, i: (b, i, 0))
    kx = pl.BlockSpec((1, n_ctx, nkv), lambda b, i: (b, 0, kcol))
    vx = pl.BlockSpec((1, n_ctx, nkv), lambda b, i: (b, 0, vcol))
    if window:
        rot3 = rot.reshape(bsz, seq, rot.shape[1])

        def kv_specs(colblk):
            return [
                pl.BlockSpec((1, BLOCK, nkv), lambda b, i: (b, jnp.maximum(i - 1, 0), colblk)),
                pl.BlockSpec((1, BLOCK, nkv), lambda b, i: (b, i, colblk)),
                pl.BlockSpec((1, BLOCK, nkv), lambda b, i: (b, jnp.minimum(i + 1, nb - 1), colblk)),
            ]
        in_specs = [smem, qspec, qspec] + kv_specs(kcol) + kv_specs(vcol) + [kx, vx]
        args = [sink, rot3, qkv3, rot3, rot3, rot3, qkv3, qkv3, qkv3, ctx3, ctx3]
    else:
        in_specs = [smem, qspec, kx, vx]
        args = [sink, qkv3, ctx3, ctx3]
    out = pl.pallas_call(
        functools.partial(_attn_kernel, window=window, nb=nb),
        grid=(bsz, nb),
        in_specs=in_specs,
        out_specs=pl.BlockSpec((1, BLOCK, nq), lambda b, i: (b, i, 0)),
        out_shape=jax.ShapeDtypeStruct((bsz, seq, nq), BF16),
        compiler_params=_cparams(("arbitrary", "arbitrary")),
        name="attn_window" if window else "attn_ctx",
    )(*args)
    return out.reshape(bsz * seq, nq)


def _outproj_ln_kernel(a_ref, w_ref, h_ref, gate_ref, g_ref, b_ref, o_ref, *, alpha):
    sub = a_ref.shape[0] // ROW_SPLIT
    for k in range(ROW_SPLIT):
        rows = slice(k * sub, (k + 1) * sub)
        y = _dot(a_ref[rows, :], w_ref[...])
        v = alpha * h_ref[rows, :] + gate_ref[0] * y
        o_ref[rows, :] = _layer_norm_rows(v, g_ref[...], b_ref[...])


def _outproj_ln_call(a, w, h, modt, ln_g, ln_b, *, alpha, seq, ctx_row, tm):
    t, k = a.shape
    d = w.shape[1]
    return pl.pallas_call(
        functools.partial(_outproj_ln_kernel, alpha=alpha),
        grid=(t // tm,),
        in_specs=[
            pl.BlockSpec((tm, k), lambda i: (i, 0)),
            pl.BlockSpec((k, d), lambda i: (0, 0)),
            pl.BlockSpec((tm, d), lambda i: (i, 0)),
            pl.BlockSpec((1, 1, d), _mod_index(2, tm, seq, ctx_row)),
            pl.BlockSpec((1, d), lambda i: (0, 0)),
            pl.BlockSpec((1, d), lambda i: (0, 0)),
        ],
        out_specs=pl.BlockSpec((tm, d), lambda i: (i, 0)),
        out_shape=jax.ShapeDtypeStruct((t, d), F32),
        compiler_params=_cparams(("arbitrary",)),
        name="outproj_ln",
    )(a, w, h, modt, ln_g.reshape(1, d), ln_b.reshape(1, d))


def _mlp_ln_kernel(h_ref, sh_ref, sc_ref, gate_ref, w1_ref, w2_ref, g_ref, b_ref, o_ref, u_scr, acc_scr, *, alpha):
    j = pl.program_id(1)

    @pl.when(j == 0)
    def _():
        u_scr[...] = (h_ref[...] * (1.0 + sc_ref[0]) + sh_ref[0]).astype(BF16)
        acc_scr[...] = jnp.zeros_like(acc_scr)

    last = pl.num_programs(1) - 1

    def hidden(rows):
        a = jnp.maximum(_dot(u_scr[rows, :], w1_ref[...]), 0.0)
        return _dot((a * a).astype(BF16), w2_ref[...])

    @pl.when(j < last)
    def _():
        acc_scr[...] += hidden(slice(None))

    @pl.when(j == last)
    def _():
        sub = h_ref.shape[0] // ROW_SPLIT
        for k in range(ROW_SPLIT):
            rows = slice(k * sub, (k + 1) * sub)
            v = alpha * h_ref[rows, :] + gate_ref[0] * (acc_scr[rows, :] + hidden(rows))
            o_ref[rows, :] = _layer_norm_rows(v, g_ref[...], b_ref[...])


def _mlp_ln_call(h, modt, w1, w2, ln_g, ln_b, *, alpha, seq, ctx_row, tm, tf):
    t, d = h.shape
    ff = w1.shape[1]
    return pl.pallas_call(
        functools.partial(_mlp_ln_kernel, alpha=alpha),
        grid=(t // tm, ff // tf),
        in_specs=[
            pl.BlockSpec((tm, d), lambda i, j: (i, 0)),
            pl.BlockSpec((1, 1, d), _mod_index(3, tm, seq, ctx_row)),
            pl.BlockSpec((1, 1, d), _mod_index(4, tm, seq, ctx_row)),
            pl.BlockSpec((1, 1, d), _mod_index(5, tm, seq, ctx_row)),
            pl.BlockSpec((d, tf), lambda i, j: (0, j)),
            pl.BlockSpec((tf, d), lambda i, j: (j, 0)),
            pl.BlockSpec((1, d), lambda i, j: (0, 0)),
            pl.BlockSpec((1, d), lambda i, j: (0, 0)),
        ],
        out_specs=pl.BlockSpec((tm, d), lambda i, j: (i, 0)),
        out_shape=jax.ShapeDtypeStruct((t, d), F32),
        scratch_shapes=[pltpu.VMEM((tm, d), BF16), pltpu.VMEM((tm, d), F32)],
        compiler_params=_cparams(("arbitrary", "arbitrary")),
        name="mlp_ln",
    )(h, modt, modt, modt, w1, w2, ln_g.reshape(1, d), ln_b.reshape(1, d))


SUBLANES = 8
PERM_BLOCK = SSM_CHUNK
PERM_VREGS = PERM_BLOCK // SUBLANES


def _token_of_row(r):
    return (r & (SUBLANES - 1)) * PERM_VREGS + (r >> 3)


def _perm_matrix(inverse):
    rr = lax.broadcasted_iota(jnp.int32, (PERM_BLOCK, PERM_BLOCK), 0)
    cc = lax.broadcasted_iota(jnp.int32, (PERM_BLOCK, PERM_BLOCK), 1)
    hit = (rr == _token_of_row(cc)) if inverse else (cc == _token_of_row(rr))
    return jnp.where(hit, 1.0, 0.0).astype(BF16)


def _ssm_inproj_kernel(h_ref, sh_ref, sc_ref, wg_ref, w_ref, wdt_ref, dtb_ref, cw_ref, cb_ref,
                       gate_ref, o_ref, dt_ref, u_scr, *, n_dt, n_gate_tiles):
    j = pl.program_id(1)
    seq = h_ref.shape[0]
    tile = o_ref.shape[2]
    nblk = seq // PERM_BLOCK
    nv = PERM_VREGS

    @pl.when(j == 0)
    def _():
        perm = _perm_matrix(False)
        for k in range(nblk):
            rows = slice(k * PERM_BLOCK, (k + 1) * PERM_BLOCK)
            u = (h_ref[rows, :] * (1.0 + sc_ref[0]) + sh_ref[0]).astype(BF16)
            u_scr[rows, :] = _dot(perm, u).astype(BF16)
        raw = _dot(u_scr[...], wdt_ref[...]) + dtb_ref[...]
        sp = jnp.maximum(raw, 0.0) + jnp.log1p(jnp.exp(-jnp.abs(raw)))
        lane = lax.broadcasted_iota(jnp.int32, sp.shape, 1)
        dt_ref[...] = jnp.where(lane < n_dt, sp, 0.0)

    def conv_tile(with_gate):
        w = cw_ref[0]
        bias = cb_ref[0]
        row = lax.broadcasted_iota(jnp.int32, (SUBLANES, tile), 0)
        zeros = jnp.zeros((SUBLANES, tile), F32)
        blocks = [None] * nblk

        def up(v, nxt):
            return jnp.where(row == SUBLANES - 1, pltpu.roll(nxt, SUBLANES - 1, 0), pltpu.roll(v, SUBLANES - 1, 0))

        def down(v, prv):
            return jnp.where(row == 0, pltpu.roll(prv, 1, 0), pltpu.roll(v, 1, 0))

        def conv(k):
            x = blocks[k]
            nxt = blocks[k + 1] if k + 1 < nblk else None
            prv = blocks[k - 1] if k > 0 else None
            u0 = up(x[0], zeros if nxt is None else nxt[0])
            u1 = up(x[1], zeros if nxt is None else nxt[1])
            d0 = down(x[nv - 2], zeros if prv is None else prv[nv - 2])
            d1 = down(x[nv - 1], zeros if prv is None else prv[nv - 1])
            shifted = (
                jnp.concatenate([d0[None], d1[None], x[:nv - 2]], axis=0),
                jnp.concatenate([d1[None], x[:nv - 1]], axis=0),
                x,
                jnp.concatenate([x[1:], u0[None]], axis=0),
                jnp.concatenate([x[2:], u0[None], u1[None]], axis=0),
            )
            acc = jnp.zeros((nv, SUBLANES, tile), F32) + bias
            for t in range(SSM_CONV_W):
                acc = acc + w[t:t + 1] * shifted[t]
            res = (acc * _sigmoid(acc)).astype(BF16)
            o_ref[0, k * PERM_BLOCK:(k + 1) * PERM_BLOCK, :] = res.reshape(PERM_BLOCK, tile)

        def project(k):
            rows = slice(k * PERM_BLOCK, (k + 1) * PERM_BLOCK)
            blocks[k] = _dot(u_scr[rows, :], w_ref[...]).reshape(nv, SUBLANES, tile)
            if with_gate:
                r = _dot(u_scr[rows, :], wg_ref[...])
                gate_ref[0, rows, :] = (r * _sigmoid(r)).astype(BF16)

        for k in range(min(2, nblk)):
            project(k)
        for k in range(nblk):
            if k + 2 < nblk:
                project(k + 2)
            conv(k)

    @pl.when(j < n_gate_tiles)
    def _():
        conv_tile(True)

    @pl.when(j >= n_gate_tiles)
    def _():
        conv_tile(False)


def _ssm_inproj_call(h, modt, w_main, w_dt, dt_bias, conv_w, conv_b, *, n_dt, n_gate_tiles, bsz, seq, ctx_row, tile):
    t, d = h.shape
    n_conv_tiles = w_main.shape[1] // tile - n_gate_tiles
    assert n_conv_tiles >= n_gate_tiles
    last_gate = n_gate_tiles - 1
    return pl.pallas_call(
        functools.partial(_ssm_inproj_kernel, n_dt=n_dt, n_gate_tiles=n_gate_tiles),
        grid=(bsz, n_conv_tiles),
        in_specs=[
            pl.BlockSpec((seq, d), lambda b, j: (b, 0)),
            pl.BlockSpec((1, 1, d), _mod_index(0, seq, seq, ctx_row)),
            pl.BlockSpec((1, 1, d), _mod_index(1, seq, seq, ctx_row)),
            pl.BlockSpec((d, tile), lambda b, j: (0, jnp.minimum(j, last_gate))),
            pl.BlockSpec((d, tile), lambda b, j: (0, n_gate_tiles + j)),
            pl.BlockSpec((d, LANES), lambda b, j: (0, 0)),
            pl.BlockSpec((1, LANES), lambda b, j: (0, 0)),
            pl.BlockSpec((1, 8, tile), lambda b, j: (j, 0, 0)),
            pl.BlockSpec((1, 1, tile), lambda b, j: (j, 0, 0)),
        ],
        out_specs=[
            pl.BlockSpec((1, seq, tile), lambda b, j: (jnp.minimum(j, last_gate), b, 0)),
            pl.BlockSpec((1, seq, tile), lambda b, j: (j, b, 0)),
            pl.BlockSpec((seq, LANES), lambda b, j: (b, 0)),
        ],
        out_shape=[
            jax.ShapeDtypeStruct((n_gate_tiles, t, tile), BF16),
            jax.ShapeDtypeStruct((n_conv_tiles, t, tile), BF16),
            jax.ShapeDtypeStruct((t, LANES), F32),
        ],
        scratch_shapes=[pltpu.VMEM((seq, d), BF16)],
        compiler_params=_cparams(("arbitrary", "arbitrary")),
        name="ssm_inproj_conv",
    )(h, modt, modt, w_main, w_main, w_dt, dt_bias, conv_w, conv_b)


HEADS_PER_GROUP = 8
DT_LANES_PER_GROUP = 2 * HEADS_PER_GROUP


def _expand_heads(v, lane0, nrows):
    rows = max(nrows, 8)
    if rows != nrows:
        v = jnp.broadcast_to(v, (rows, LANES))
    lane = lax.broadcasted_iota(jnp.int32, (rows, LANES), 1)
    parts = []
    for p in range(HEADS_PER_GROUP // 2):
        idx = lane0 + 2 * p + (lane >> 6)
        parts.append(jnp.take_along_axis(v, idx, axis=1, mode="promise_in_bounds"))
    return jnp.concatenate(parts, axis=1)[:nrows]


def _cumsum_rows(a):
    q = a.shape[0]
    tril = (_token_of_row(lax.broadcasted_iota(jnp.int32, (q, q), 0))
            >= _token_of_row(lax.broadcasted_iota(jnp.int32, (q, q), 1)))
    tril = jnp.where(tril, 1.0, 0.0).astype(BF16)
    hi = a.astype(BF16)
    r1 = a - hi.astype(F32)
    mid = r1.astype(BF16)
    lo = (r1 - mid.astype(F32)).astype(BF16)
    return _dot(tril, hi) + _dot(tril, mid) + _dot(tril, lo)


def _chunk_terms(dt, a_neg):
    q = dt.shape[0]
    lane = lax.broadcasted_iota(jnp.int32, (q, LANES), 1)
    is_bwd = (lane & HEADS_PER_GROUP) != 0
    a = dt * a_neg
    acs = _cumsum_rows(a)
    tot = acs[q - 1:q, :]
    e = jnp.where(is_bwd, acs - a, acs)
    w_state = dt * jnp.exp(jnp.where(is_bwd, e, tot - acs))
    f_out = jnp.exp(jnp.where(is_bwd, tot - e, acs))
    dec = jnp.exp(tot)
    return e, w_state, f_out, dec


def _state_update(s_scr, idx, x, bm, w_state, dec, lane0):
    q = x.shape[0]
    xw = (x.astype(F32) * _expand_heads(w_state, lane0, q)).astype(BF16)
    s_scr[idx] = s_scr[idx] * _expand_heads(dec, lane0, 1) + _dot_tn(bm, xw)


def _ssd_kernel(alog_ref, d_ref, ng_ref, xc_ref, dtc_ref, xb_ref, dtb_ref, xf_ref, zf_ref, dtf_ref,
                y_ref, sf_scr, sb_scr, sbs_scr, *, nb, ncc, ncl):
    o = pl.program_id(0)
    s = pl.program_id(1)
    par = o % 2
    prev = 1 - par
    entering = o < nb
    emitting = o >= 1
    lat = s >= 2 * ncc
    t = s - 2 * ncc
    a_neg = -jnp.exp(alog_ref[...])
    q = SSM_CHUNK
    n = SSM_STATE
    gcols = HEADS_PER_GROUP * SSM_HEAD_DIM

    def sweep_steps(s_scr, slot, x_ref, dt_ref, dir_off, keep=None):
        _, w_state, _, dec = _chunk_terms(dt_ref[...], a_neg)

        def one(g):
            idx = g if slot is None else (slot, g)
            if keep is not None:
                sbs_scr[par, keep, g] = s_scr[idx].astype(BF16)
            _state_update(s_scr, idx, x_ref[g], x_ref[SSM_GROUPS, :, g * n:(g + 1) * n], w_state, dec,
                          g * DT_LANES_PER_GROUP + dir_off)
        return [functools.partial(one, g) for g in range(SSM_GROUPS)]

    def sweep(*args, **kwargs):
        for step in sweep_steps(*args, **kwargs):
            step()

    @pl.when((s == 0) & entering)
    def _():
        sf_scr[par] = jnp.zeros(sf_scr.shape[1:], F32)
        sb_scr[...] = jnp.zeros_like(sb_scr)

    @pl.when((s < ncc) & entering)
    def _():
        sweep(sb_scr, None, xc_ref, dtc_ref, HEADS_PER_GROUP)

    @pl.when((s >= ncc) & (s < 2 * ncc) & entering)
    def _():
        sweep(sf_scr, par, xc_ref, dtc_ref, 0)

    def backward_steps():
        return sweep_steps(sb_scr, None, xb_ref, dtb_ref, HEADS_PER_GROUP, keep=ncl - 1 - t)

    def forward(between=None):
        dt = dtf_ref[...]
        e, w_state, f_out, dec = _chunk_terms(dt, a_neg)
        e_t = e.T
        dt_t = dt.T
        li = _token_of_row(lax.broadcasted_iota(jnp.int32, (q, q), 0))
        si = _token_of_row(lax.broadcasted_iota(jnp.int32, (q, q), 1))
        unperm = _perm_matrix(True)
        lower = li >= si
        below = li > si
        above = li < si
        sign = jnp.where(lower, 1.0, -1.0)
        dir_lane = jnp.where(lower, 0, HEADS_PER_GROUP)
        first = lax.broadcasted_iota(jnp.int32, (q, LANES), 1) < SSM_HEAD_DIM
        for g in range(SSM_GROUPS):
            lf = g * DT_LANES_PER_GROUP
            lb = lf + HEADS_PER_GROUP
            x = xf_ref[g]
            bm = xf_ref[SSM_GROUPS, :, g * n:(g + 1) * n]
            cm = xf_ref[SSM_GROUPS + 1, :, g * n:(g + 1) * n]
            cb = _dot_nt(cm, bm)
            y_f = _dot(cm, sf_scr[prev, g].astype(BF16))
            y_b = _dot(cm, sbs_scr[prev, t, g])
            y_parts = []
            for p in range(HEADS_PER_GROUP // 2):
                ms = []
                for r in (2 * p, 2 * p + 1):
                    e_col = jnp.take_along_axis(e, dir_lane + (lf + r), axis=1, mode="promise_in_bounds")
                    e_row = jnp.where(lower, e_t[lf + r:lf + r + 1, :], e_t[lb + r:lb + r + 1, :])
                    dt_f = dt_t[lf + r:lf + r + 1, :]
                    dt_b = dt_t[lb + r:lb + r + 1, :]
                    dsel = jnp.where(below, dt_f, jnp.where(above, dt_b, dt_f + dt_b))
                    ms.append((cb * jnp.exp((e_col - e_row) * sign) * dsel).astype(BF16))
                xp = x[:, p * LANES:(p + 1) * LANES]
                zero = jnp.zeros_like(xp)
                x_bd = jnp.concatenate([jnp.where(first, xp, zero), jnp.where(first, zero, xp)], axis=0)
                y_parts.append(_dot(jnp.concatenate(ms, axis=1), x_bd))
            cols = slice(g * gcols, (g + 1) * gcols)
            y = (jnp.concatenate(y_parts, axis=1)
                 + y_f * _expand_heads(f_out, lf, q) + y_b * _expand_heads(f_out, lb, q)
                 + x.astype(F32) * d_ref[:, cols])
            yy = y * zf_ref[g].astype(F32)
            ms_ = jnp.mean(yy * yy, axis=-1, keepdims=True)
            y_out = (yy * lax.rsqrt(ms_ + RMS_EPS) * ng_ref[:, cols]).astype(BF16)
            y_ref[:, cols] = _dot(unperm, y_out).astype(BF16)
            _state_update(sf_scr, (prev, g), x, bm, w_state, dec, lf)
            if between is not None:
                between[g]()

    @pl.when(lat & entering & emitting)
    def _():
        forward(between=backward_steps())

    @pl.when(lat & jnp.logical_not(emitting))
    def _():
        for step in backward_steps():
            step()

    @pl.when(lat & jnp.logical_not(entering))
    def _():
        forward()


def _ssd_call(a_log, d_row, norm_g, xbc_c, dt_c, gate_l, xbc_l, dt_l, *, bsz, n_ctx, seq):
    q = SSM_CHUNK
    ncc = n_ctx // q
    ncl = seq // q
    gcols = xbc_l.shape[2]
    d_inner = SSM_GROUPS * gcols
    nsteps = 2 * ncc + ncl
    n_state = SSM_STATE

    def entering(o):
        return jnp.minimum(o, bsz - 1)

    def emitting(o):
        return jnp.maximum(o - 1, 0)

    def cchunk(s):
        return jnp.where(s < ncc, ncc - 1 - s, jnp.where(s < 2 * ncc, s - ncc, ncc - 1))

    def bchunk(s):
        return ncl - 1 - jnp.maximum(s - 2 * ncc, 0)

    def fchunk(s):
        return jnp.maximum(s - 2 * ncc, 0)

    def crow(o, s):
        return entering(o) * ncc + cchunk(s)

    def brow(o, s):
        return entering(o) * ncl + bchunk(s)

    def frow(o, s):
        return emitting(o) * ncl + fchunk(s)

    def orow(o, s):
        return jnp.where(o == 0, 0, frow(o, s))

    assert SSM_GROUPS * n_state == gcols
    cspec = (xbc_l.shape[0], q, gcols)
    in_specs = [
        pl.BlockSpec((1, LANES), lambda o, s: (0, 0)),
        pl.BlockSpec((1, d_inner), lambda o, s: (0, 0)),
        pl.BlockSpec((1, d_inner), lambda o, s: (0, 0)),
        pl.BlockSpec(cspec, lambda o, s: (0, crow(o, s), 0)),
        pl.BlockSpec((q, LANES), lambda o, s: (crow(o, s), 0)),
        pl.BlockSpec(cspec, lambda o, s: (0, brow(o, s), 0)),
        pl.BlockSpec((q, LANES), lambda o, s: (brow(o, s), 0)),
        pl.BlockSpec(cspec, lambda o, s: (0, frow(o, s), 0)),
        pl.BlockSpec((SSM_GROUPS, q, gcols), lambda o, s: (0, frow(o, s), 0)),
        pl.BlockSpec((q, LANES), lambda o, s: (frow(o, s), 0)),
    ]
    return pl.pallas_call(
        functools.partial(_ssd_kernel, nb=bsz, ncc=ncc, ncl=ncl),
        grid=(bsz + 1, nsteps),
        in_specs=in_specs,
        out_specs=pl.BlockSpec((q, d_inner), lambda o, s: (orow(o, s), 0)),
        out_shape=jax.ShapeDtypeStruct((bsz * seq, d_inner), BF16),
        scratch_shapes=[
            pltpu.VMEM((2, SSM_GROUPS, n_state, gcols), F32),
            pltpu.VMEM((SSM_GROUPS, n_state, gcols), F32),
            pltpu.VMEM((2, ncl, SSM_GROUPS, n_state, gcols), BF16),
        ],
        compiler_params=_cparams(("arbitrary", "arbitrary")),
        name="ssd_scan",
    )(a_log, d_row, norm_g, xbc_c, dt_c, xbc_l, dt_l, xbc_l, gate_l, dt_l)


def _rope_tables(seq):
    rows = seq // GRID_W
    row = jnp.repeat(jnp.arange(rows), GRID_W).astype(F32)
    col = jnp.tile(jnp.arange(GRID_W), rows).astype(F32)
    inv_freq = ROPE_BASE ** (-jnp.arange(0, ROPE_AXIS_DIM, 2, dtype=F32) / ROPE_AXIS_DIM)
    ang_r = row[:, None] * inv_freq[None, :]
    ang_c = col[:, None] * inv_freq[None, :]
    cos = jnp.concatenate([jnp.cos(ang_r)] * 2 + [jnp.cos(ang_c)] * 2, axis=-1)
    sin = jnp.concatenate([-jnp.sin(ang_r), jnp.sin(ang_r), -jnp.sin(ang_c), jnp.sin(ang_c)], axis=-1)
    reps = LANES // HEAD_DIM
    return jnp.tile(cos, (1, reps)), jnp.tile(sin, (1, reps))


def _dup_heads(w):
    d, n = w.shape
    w = w.reshape(d, n // HEAD_DIM, 1, HEAD_DIM)
    return jnp.broadcast_to(w, (d, n // HEAD_DIM, LANES // HEAD_DIM, HEAD_DIM)).reshape(d, -1)


def _attention_layer(h_lat, h_ctx, modt, w_in, w_out, sink, ln_g, ln_b, *, bsz, seq, n_ctx, ctx_row, alpha, tm):
    nq = ATT_HEADS * HEAD_DIM
    nk = ATT_KV_HEADS * HEAD_DIM
    wq, wk, wv = w_in[:, :nq], w_in[:, nq:nq + nk], w_in[:, nq + nk:]
    wk, wv = _dup_heads(wk), _dup_heads(wv)
    w_cat = jnp.concatenate([wq, wk, wv], axis=1).astype(BF16)
    nrot = nq + wk.shape[1]
    tables = _rope_tables(seq)
    qkv_l, rot_l = _att_inproj_call(h_lat, modt, w_cat, tables, seq=seq, ctx_row=None, nq=nq, nrot=nrot, tm=tm)
    (qkv_c,) = _att_inproj_call(h_ctx, modt, w_cat, None, seq=n_ctx, ctx_row=ctx_row, nq=nq, nrot=nrot,
                                tm=min(tm, h_ctx.shape[0]))
    o_l = _attn_call(sink, qkv_l, rot_l, qkv_c, bsz=bsz, seq=seq, n_ctx=n_ctx, window=True)
    o_c = _attn_call(sink, qkv_c, None, qkv_c, bsz=bsz, seq=n_ctx, n_ctx=n_ctx, window=False)
    w_o = w_out.astype(BF16)
    h_lat = _outproj_ln_call(o_l, w_o, h_lat, modt, ln_g, ln_b, alpha=alpha, seq=seq, ctx_row=None, tm=tm)
    h_ctx = _outproj_ln_call(o_c, w_o, h_ctx, modt, ln_g, ln_b, alpha=alpha, seq=n_ctx, ctx_row=ctx_row,
                             tm=min(tm, h_ctx.shape[0]))
    return h_lat, h_ctx


def _ssm_layer(h_lat, h_ctx, modt, w_in, conv_w, conv_b, dt_bias, a_log, d_skip, norm_g, w_out, ln_g, ln_b,
               *, bsz, seq, n_ctx, ctx_row, alpha, tm):
    d_inner = w_out.shape[0]
    heads = d_inner // SSM_HEAD_DIM
    gcols = d_inner // SSM_GROUPS
    hpg = heads // SSM_GROUPS
    conv_dim = d_inner + 2 * SSM_GROUPS * SSM_STATE
    n_main = d_inner + conv_dim
    assert hpg == HEADS_PER_GROUP and gcols == SSM_GROUPS * SSM_STATE
    w_main = w_in[:, :n_main].astype(BF16)

    def regroup(v):
        lead = v.shape[:-2]
        v = v.reshape(lead + (2, SSM_GROUPS, hpg))
        v = jnp.moveaxis(v, -3, -2).reshape(lead + (SSM_GROUPS, 2 * hpg))
        return v

    w_dt = regroup(w_in[:, n_main:].reshape(-1, 2, heads)).reshape(-1, 2 * heads)
    w_dt = jnp.pad(w_dt, ((0, 0), (0, LANES - 2 * heads))).astype(BF16)
    dtb = jnp.pad(regroup(dt_bias).reshape(1, 2 * heads), ((0, 0), (0, LANES - 2 * heads)))
    a_log_row = jnp.pad(regroup(a_log).reshape(1, 2 * heads), ((0, 0), (0, LANES - 2 * heads)))
    d_row = jnp.repeat(d_skip, SSM_HEAD_DIM).reshape(1, d_inner)
    ng = norm_g.reshape(1, d_inner)
    ntile_conv = conv_dim // gcols
    cw = jnp.pad(conv_w, ((0, 8 - SSM_CONV_W), (0, 0))).reshape(8, ntile_conv, gcols).transpose(1, 0, 2)
    cb = conv_b.reshape(ntile_conv, 1, gcols)

    n_gate = d_inner // gcols
    gate_l, xbc_l, dt_l = _ssm_inproj_call(h_lat, modt, w_main, w_dt, dtb, cw, cb, n_dt=2 * heads,
                                           n_gate_tiles=n_gate, bsz=bsz, seq=seq, ctx_row=None, tile=gcols)
    _, xbc_c, dt_c = _ssm_inproj_call(h_ctx, modt, w_main, w_dt, dtb, cw, cb, n_dt=2 * heads,
                                      n_gate_tiles=n_gate, bsz=bsz, seq=n_ctx, ctx_row=ctx_row, tile=gcols)
    y = _ssd_call(a_log_row, d_row, ng, xbc_c, dt_c, gate_l, xbc_l, dt_l, bsz=bsz, n_ctx=n_ctx, seq=seq)
    return _outproj_ln_call(y, w_out.astype(BF16), h_lat, modt, ln_g, ln_b, alpha=alpha, seq=seq, ctx_row=None, tm=tm)


def kernel(x, c, ctx, c_ctx, w_mod, b_mod, ln_mix_g, ln_mix_b, ln_ff_g, ln_ff_b, att_w_in, att_w_out, att_sink,
           ssm_w_in, ssm_conv_w, ssm_conv_b, ssm_dt_bias, ssm_a_log, ssm_d, ssm_norm_g, ssm_w_out, ff_w1, ff_w2):
    bsz, seq, d = x.shape
    n_ctx = ctx.shape[1]
    depth = w_mod.shape[0]
    alpha = (2.0 * depth) ** 0.25
    tm = 1024 if seq % 1024 == 0 and (bsz * n_ctx) % 1024 == 0 else 512
    ctx_row = bsz
    rows = -(-(bsz + 1) // MOD_ROWS_ALIGN) * MOD_ROWS_ALIGN
    cc = jnp.concatenate([c, c_ctx[None, :], jnp.zeros((rows - bsz - 1, d), F32)], axis=0)
    mod = _mod_call(cc, w_mod, b_mod)
    h_lat = x.reshape(bsz * seq, d)
    h_ctx = ctx.reshape(bsz * n_ctx, d)
    for i in range(depth):
        last = i == depth - 1
        j = i // N_MIXERS
        modt = mod[i].reshape(rows * N_MOD, 1, d)
        if i % N_MIXERS == 0:
            h_lat, h_ctx_mix = _attention_layer(h_lat, h_ctx, modt, att_w_in[j], att_w_out[j], att_sink[j],
                                                ln_mix_g[i], ln_mix_b[i], bsz=bsz, seq=seq, n_ctx=n_ctx,
                                                ctx_row=ctx_row, alpha=alpha, tm=tm)
        else:
            if not last:
                raise NotImplementedError("context outputs of an SSD layer are only needed when it is not last")
            h_lat = _ssm_layer(h_lat, h_ctx, modt, ssm_w_in[j], ssm_conv_w[j], ssm_conv_b[j], ssm_dt_bias[j],
                               ssm_a_log[j], ssm_d[j], ssm_norm_g[j], ssm_w_out[j], ln_mix_g[i], ln_mix_b[i],
                               bsz=bsz, seq=seq, n_ctx=n_ctx, ctx_row=ctx_row, alpha=alpha, tm=tm)
            h_ctx_mix = None
        w1 = ff_w1[i].astype(BF16)
        w2 = ff_w2[i].astype(BF16)
        h_lat = _mlp_ln_call(h_lat, modt, w1, w2, ln_ff_g[i], ln_ff_b[i], alpha=alpha, seq=seq, ctx_row=None,
                             tm=tm, tf=1024)
        if not last:
            h_ctx = _mlp_ln_call(h_ctx_mix, modt, w1, w2, ln_ff_g[i], ln_ff_b[i], alpha=alpha, seq=n_ctx,
                                 ctx_row=ctx_row, tm=min(tm, h_ctx_mix.shape[0]), tf=1024)
    return h_lat.reshape(bsz, seq, d)
```

```python
import functools

import jax
import jax.numpy as jnp
from jax import lax
from jax.experimental import pallas as pl
from jax.experimental.pallas import tpu as pltpu

F32 = jnp.float32
BF16 = jnp.bfloat16

GRID_W = 64
N_MIXERS = 2
ATT_HEADS = 16
ATT_KV_HEADS = 4
HEAD_DIM = 64
ATT_GROUP = ATT_HEADS // ATT_KV_HEADS
WINDOW = 128
BLOCK = 128
ROPE_BASE = 10000.0
ROPE_AXIS_DIM = HEAD_DIM // 2

SSM_HEAD_DIM = 64
SSM_GROUPS = 4
SSM_STATE = 128
SSM_CONV_W = 5
SSM_CHUNK = 128

LOG2E = 1.4426950408889634
Q_SCALE = HEAD_DIM ** -0.5 * LOG2E

N_MOD = 6
LN_EPS = 1e-5
RMS_EPS = 1e-5
NEG_INF = -1e30

LANES = 128
MOD_ROWS_ALIGN = 16
ROW_SPLIT = 4
VMEM_LIMIT = 48 * 1024 * 1024


def _cparams(sem):
    return pltpu.CompilerParams(dimension_semantics=sem, vmem_limit_bytes=VMEM_LIMIT)


def _sigmoid(x):
    return 1.0 / (1.0 + jnp.exp(-x))


def _dot(a, b):
    return jnp.dot(a, b, preferred_element_type=F32)


def _dot_nt(a, b):
    return lax.dot_general(a, b, (((1,), (1,)), ((), ())), preferred_element_type=F32)


def _dot_tn(a, b):
    return lax.dot_general(a, b, (((0,), (0,)), ((), ())), preferred_element_type=F32)


def _layer_norm_rows(v, g, b):
    mu = jnp.mean(v, axis=-1, keepdims=True)
    d = v - mu
    var = jnp.mean(d * d, axis=-1, keepdims=True)
    return d * lax.rsqrt(var + LN_EPS) * g + b


def _mod_kernel(c_ref, w_ref, b_ref, o_ref):
    c = c_ref[...]
    s = (c * _sigmoid(c)).astype(BF16)
    o_ref[0] = _dot(s, w_ref[0].astype(BF16)) + b_ref[0]


def _mod_call(cc, w_mod, b_mod):
    depth, d, n = w_mod.shape
    rows = cc.shape[0]
    tn = 1024
    return pl.pallas_call(
        _mod_kernel,
        grid=(depth, n // tn),
        in_specs=[
            pl.BlockSpec((rows, d), lambda l, j: (0, 0)),
            pl.BlockSpec((1, d, tn), lambda l, j: (l, 0, j)),
            pl.BlockSpec((1, 1, tn), lambda l, j: (l, 0, j)),
        ],
        out_specs=pl.BlockSpec((1, rows, tn), lambda l, j: (l, 0, j)),
        out_shape=jax.ShapeDtypeStruct((depth, rows, n), F32),
        compiler_params=_cparams(("arbitrary", "arbitrary")),
        name="adaln_mod",
    )(cc, w_mod, b_mod.reshape(depth, 1, n))


def _mod_index(which, tm, seq, ctx_row):
    if ctx_row is None:
        return lambda i, *_: ((i * tm // seq) * N_MOD + which, 0, 0)
    return lambda i, *_: (ctx_row * N_MOD + which, 0, 0)


def _att_inproj_kernel(*refs, rope, nq):
    if rope:
        x_ref, sh_ref, sc_ref, w_ref, cos_ref, sin_ref, qkv_ref, rot_ref = refs
    else:
        x_ref, sh_ref, sc_ref, w_ref, qkv_ref = refs
    u = (x_ref[...] * (1.0 + sc_ref[0]) + sh_ref[0]).astype(BF16)
    acc = _dot(u, w_ref[...])
    n = acc.shape[1]
    nrot = rot_ref.shape[1] if rope else 0
    if rope:
        cos = cos_ref[...]
        sin = sin_ref[...]
        lane = lax.broadcasted_iota(jnp.int32, cos.shape, 1)
        first_half = (lane & (ROPE_AXIS_DIM - 1)) < (ROPE_AXIS_DIM // 2)
    for cidx in range(n // LANES):
        chunk = acc[:, cidx * LANES:(cidx + 1) * LANES]
        if cidx * LANES < nq:
            chunk = chunk * Q_SCALE
        qkv_ref[:, cidx * LANES:(cidx + 1) * LANES] = chunk.astype(BF16)
        if rope and cidx * LANES < nrot:
            half = ROPE_AXIS_DIM // 2
            swapped = jnp.where(first_half, pltpu.roll(chunk, LANES - half, 1), pltpu.roll(chunk, half, 1))
            rot_ref[:, cidx * LANES:(cidx + 1) * LANES] = (chunk * cos + swapped * sin).astype(BF16)


def _att_inproj_call(h, modt, w, tables, *, seq, ctx_row, nq, nrot, tm):
    t, d = h.shape
    n = w.shape[1]
    rope = tables is not None
    in_specs = [
        pl.BlockSpec((tm, d), lambda i: (i, 0)),
        pl.BlockSpec((1, 1, d), _mod_index(0, tm, seq, ctx_row)),
        pl.BlockSpec((1, 1, d), _mod_index(1, tm, seq, ctx_row)),
        pl.BlockSpec((d, n), lambda i: (0, 0)),
    ]
    args = [h, modt, modt, w]
    out_specs = [pl.BlockSpec((tm, n), lambda i: (i, 0))]
    out_shape = [jax.ShapeDtypeStruct((t, n), BF16)]
    if rope:
        per_seq = seq // tm
        in_specs += [pl.BlockSpec((tm, LANES), lambda i: (i % per_seq, 0))] * 2
        args += list(tables)
        out_specs.append(pl.BlockSpec((tm, nrot), lambda i: (i, 0)))
        out_shape.append(jax.ShapeDtypeStruct((t, nrot), BF16))
    return pl.pallas_call(
        functools.partial(_att_inproj_kernel, rope=rope, nq=nq),
        grid=(t // tm,),
        in_specs=in_specs,
        out_specs=out_specs,
        out_shape=out_shape,
        compiler_params=_cparams(("arbitrary",)),
        name="att_inproj_rope" if rope else "att_inproj",
    )(*args)


def _attn_kernel(*refs, window, nb):
    if window:
        (sink_ref, qr_ref, qp_ref, kp_ref, kc_ref, kn_ref, vp_ref, vc_ref, vn_ref,
         kx_ref, vx_ref, o_ref) = refs
    else:
        sink_ref, qp_ref, kx_ref, vx_ref, o_ref = refs
    i = pl.program_id(1)
    rows = ATT_GROUP * BLOCK
    half_lane = lax.broadcasted_iota(jnp.int32, (BLOCK, LANES), 1) < HEAD_DIM
    ridx = lax.broadcasted_iota(jnp.int32, (rows, 1), 0)
    if window:
        li = lax.broadcasted_iota(jnp.int32, (BLOCK, BLOCK), 0)
        ci = lax.broadcasted_iota(jnp.int32, (BLOCK, BLOCK), 1)
        ok_prev = ci >= li + jnp.where(i > 0, 0, BLOCK)
        ok_next = ci + jnp.where(i < nb - 1, 0, BLOCK) <= li

    def masked(s_blk, ok):
        return jnp.concatenate([jnp.where(ok, s_blk[r * BLOCK:(r + 1) * BLOCK], NEG_INF)
                                for r in range(ATT_GROUP)], axis=0)

    def with_ones(v):
        return jnp.concatenate([v, jnp.ones_like(v)], axis=1)

    def stack_heads(q_ref, g):
        parts = []
        for p in range(ATT_GROUP // 2):
            c0 = (g * (ATT_GROUP // 2) + p) * LANES
            pair = q_ref[0, :, c0:c0 + LANES]
            zero = jnp.zeros_like(pair)
            parts.append(jnp.where(half_lane, pair, zero))
            parts.append(jnp.where(half_lane, zero, pair))
        return jnp.concatenate(parts, axis=0)

    def scores(g):
        gl = slice(g * LANES, (g + 1) * LANES)
        s_ctx = _dot_nt(stack_heads(qp_ref, g), kx_ref[0, :, gl])
        if not window:
            return s_ctx, None
        kw = jnp.concatenate([kp_ref[0, :, gl], kc_ref[0, :, gl], kn_ref[0, :, gl]], axis=0)
        s_win = _dot_nt(stack_heads(qr_ref, g), kw)
        s_win = jnp.concatenate([masked(s_win[:, :BLOCK], ok_prev), s_win[:, BLOCK:2 * BLOCK],
                                 masked(s_win[:, 2 * BLOCK:], ok_next)], axis=1)
        return s_ctx, s_win

    ahead = scores(0)
    for g in range(ATT_KV_HEADS):
        s_ctx, s_win = ahead
        if g + 1 < ATT_KV_HEADS:
            ahead = scores(g + 1)
        gl = slice(g * LANES, (g + 1) * LANES)
        sink_col = jnp.full((rows, 1), sink_ref[g * ATT_GROUP + ATT_GROUP - 1] * LOG2E, F32)
        for r in range(ATT_GROUP - 2, -1, -1):
            sink_col = jnp.where(ridx < (r + 1) * BLOCK, sink_ref[g * ATT_GROUP + r] * LOG2E, sink_col)
        pieces = [s_ctx[:, k * LANES:(k + 1) * LANES] for k in range(s_ctx.shape[1] // LANES)]
        if window:
            pieces += [s_win[:, k * BLOCK:(k + 1) * BLOCK] for k in range(3)]
        mx = pieces[0]
        for piece in pieces[1:]:
            mx = jnp.maximum(mx, piece)
        m = jnp.maximum(jnp.max(mx, axis=-1, keepdims=True), sink_col)
        p_ctx = jnp.exp2(s_ctx - m)
        if window:
            vw = jnp.concatenate([vp_ref[0, :, gl], vc_ref[0, :, gl], vn_ref[0, :, gl]], axis=0)
            acc = (_dot(p_ctx.astype(BF16), with_ones(vx_ref[0, :, gl]))
                   + _dot(jnp.exp2(s_win - m).astype(BF16), with_ones(vw)))
            denom = acc[:, LANES:LANES + 1] + jnp.exp2(sink_col - m)
            o = acc[:, :LANES] * (1.0 / denom)
        else:
            denom = jnp.sum(p_ctx, axis=-1, keepdims=True) + jnp.exp2(sink_col - m)
            o = _dot(p_ctx.astype(BF16), vx_ref[0, :, gl]) * (1.0 / denom)
        for p in range(ATT_GROUP // 2):
            a = o[(2 * p) * BLOCK:(2 * p + 1) * BLOCK]
            b = o[(2 * p + 1) * BLOCK:(2 * p + 2) * BLOCK]
            c0 = (g * (ATT_GROUP // 2) + p) * LANES
            o_ref[0, :, c0:c0 + LANES] = jnp.where(half_lane, a, b).astype(BF16)


def _attn_call(sink, qkv, rot, qkv_ctx, *, bsz, seq, n_ctx, window):
    nq = ATT_HEADS * HEAD_DIM
    nkv = ATT_KV_HEADS * LANES
    nb = seq // BLOCK
    qkv3 = qkv.reshape(bsz, seq, qkv.shape[1])
    ctx3 = qkv_ctx.reshape(bsz, n_ctx, qkv_ctx.shape[1])
    kcol = nq // nkv
    vcol = kcol + 1
    smem = pl.BlockSpec(memory_space=pltpu.SMEM)
    qspec = pl.BlockSpec((1, BLOCK, nq), lambda b, i: (b, i, 0))
    kx = pl.BlockSpec((1, n_ctx, nkv), lambda b, i: (b, 0, kcol))
    vx = pl.BlockSpec((1, n_ctx, nkv), lambda b, i: (b, 0, vcol))
    if window:
        rot3 = rot.reshape(bsz, seq, rot.shape[1])

        def kv_specs(colblk):
            return [
                pl.BlockSpec((1, BLOCK, nkv), lambda b, i: (b, jnp.maximum(i - 1, 0), colblk)),
                pl.BlockSpec((1, BLOCK, nkv), lambda b, i: (b, i, colblk)),
                pl.BlockSpec((1, BLOCK, nkv), lambda b, i: (b, jnp.minimum(i + 1, nb - 1), colblk)),
            ]
        in_specs = [smem, qspec, qspec] + kv_specs(kcol) + kv_specs(vcol) + [kx, vx]
        args = [sink, rot3, qkv3, rot3, rot3, rot3, qkv3, qkv3, qkv3, ctx3, ctx3]
    else:
        in_specs = [smem, qspec, kx, vx]
        args = [sink, qkv3, ctx3, ctx3]
    out = pl.pallas_call(
        functools.partial(_attn_kernel, window=window, nb=nb),
        grid=(bsz, nb),
        in_specs=in_specs,
        out_specs=pl.BlockSpec((1, BLOCK, nq), lambda b, i: (b, i, 0)),
        out_shape=jax.ShapeDtypeStruct((bsz, seq, nq), BF16),
        compiler_params=_cparams(("arbitrary", "arbitrary")),
        name="attn_window" if window else "attn_ctx",
    )(*args)
    return out.reshape(bsz * seq, nq)


def _outproj_ln_kernel(a_ref, w_ref, h_ref, gate_ref, g_ref, b_ref, o_ref, *, alpha):
    sub = a_ref.shape[0] // ROW_SPLIT
    for k in range(ROW_SPLIT):
        rows = slice(k * sub, (k + 1) * sub)
        y = _dot(a_ref[rows, :], w_ref[...])
        v = alpha * h_ref[rows, :] + gate_ref[0] * y
        o_ref[rows, :] = _layer_norm_rows(v, g_ref[...], b_ref[...])


def _outproj_ln_call(a, w, h, modt, ln_g, ln_b, *, alpha, seq, ctx_row, tm):
    t, k = a.shape
    d = w.shape[1]
    return pl.pallas_call(
        functools.partial(_outproj_ln_kernel, alpha=alpha),
        grid=(t // tm,),
        in_specs=[
            pl.BlockSpec((tm, k), lambda i: (i, 0)),
            pl.BlockSpec((k, d), lambda i: (0, 0)),
            pl.BlockSpec((tm, d), lambda i: (i, 0)),
            pl.BlockSpec((1, 1, d), _mod_index(2, tm, seq, ctx_row)),
            pl.BlockSpec((1, d), lambda i: (0, 0)),
            pl.BlockSpec((1, d), lambda i: (0, 0)),
        ],
        out_specs=pl.BlockSpec((tm, d), lambda i: (i, 0)),
        out_shape=jax.ShapeDtypeStruct((t, d), F32),
        compiler_params=_cparams(("arbitrary",)),
        name="outproj_ln",
    )(a, w, h, modt, ln_g.reshape(1, d), ln_b.reshape(1, d))


def _mlp_ln_kernel(h_ref, sh_ref, sc_ref, gate_ref, w1_ref, w2_ref, g_ref, b_ref, o_ref, u_scr, acc_scr, *, alpha):
    j = pl.program_id(1)

    @pl.when(j == 0)
    def _():
        u_scr[...] = (h_ref[...] * (1.0 + sc_ref[0]) + sh_ref[0]).astype(BF16)
        acc_scr[...] = jnp.zeros_like(acc_scr)

    last = pl.num_programs(1) - 1

    def hidden(rows):
        a = jnp.maximum(_dot(u_scr[rows, :], w1_ref[...]), 0.0)
        return _dot((a * a).astype(BF16), w2_ref[...])

    @pl.when(j < last)
    def _():
        acc_scr[...] += hidden(slice(None))

    @pl.when(j == last)
    def _():
        sub = h_ref.shape[0] // ROW_SPLIT
        for k in range(ROW_SPLIT):
            rows = slice(k * sub, (k + 1) * sub)
            v = alpha * h_ref[rows, :] + gate_ref[0] * (acc_scr[rows, :] + hidden(rows))
            o_ref[rows, :] = _layer_norm_rows(v, g_ref[...], b_ref[...])


def _mlp_ln_call(h, modt, w1, w2, ln_g, ln_b, *, alpha, seq, ctx_row, tm, tf):
    t, d = h.shape
    ff = w1.shape[1]
    return pl.pallas_call(
        functools.partial(_mlp_ln_kernel, alpha=alpha),
        grid=(t // tm, ff // tf),
        in_specs=[
            pl.BlockSpec((tm, d), lambda i, j: (i, 0)),
            pl.BlockSpec((1, 1, d), _mod_index(3, tm, seq, ctx_row)),
            pl.BlockSpec((1, 1, d), _mod_index(4, tm, seq, ctx_row)),
            pl.BlockSpec((1, 1, d), _mod_index(5, tm, seq, ctx_row)),
            pl.BlockSpec((d, tf), lambda i, j: (0, j)),
            pl.BlockSpec((tf, d), lambda i, j: (j, 0)),
            pl.BlockSpec((1, d), lambda i, j: (0, 0)),
            pl.BlockSpec((1, d), lambda i, j: (0, 0)),
        ],
        out_specs=pl.BlockSpec((tm, d), lambda i, j: (i, 0)),
        out_shape=jax.ShapeDtypeStruct((t, d), F32),
        scratch_shapes=[pltpu.VMEM((tm, d), BF16), pltpu.VMEM((tm, d), F32)],
        compiler_params=_cparams(("arbitrary", "arbitrary")),
        name="mlp_ln",
    )(h, modt, modt, modt, w1, w2, ln_g.reshape(1, d), ln_b.reshape(1, d))


SUBLANES = 8
PERM_BLOCK = SSM_CHUNK
PERM_VREGS = PERM_BLOCK // SUBLANES
PROJ_ROWS = PERM_BLOCK


def _token_of_row(r):
    return (r & (SUBLANES - 1)) * PERM_VREGS + (r >> 3)


def _perm_matrix(inverse):
    rr = lax.broadcasted_iota(jnp.int32, (PERM_BLOCK, PERM_BLOCK), 0)
    cc = lax.broadcasted_iota(jnp.int32, (PERM_BLOCK, PERM_BLOCK), 1)
    hit = (rr == _token_of_row(cc)) if inverse else (cc == _token_of_row(rr))
    return jnp.where(hit, 1.0, 0.0).astype(BF16)


def _ssm_inproj_kernel(h_ref, sh_ref, sc_ref, wg_ref, w_ref, wdt_ref, dtb_ref, cw_ref, cb_ref,
                       gate_ref, o_ref, dt_ref, u_scr, *, n_dt, n_gate_tiles, seq):
    j = pl.program_id(1)
    rows_per_step = h_ref.shape[0]
    tile = o_ref.shape[2]
    nblk = rows_per_step // PERM_BLOCK
    seq_blocks = seq // PERM_BLOCK
    nv = PERM_VREGS

    @pl.when(j == 0)
    def _():
        perm = _perm_matrix(False)
        for k in range(nblk):
            rows = slice(k * PERM_BLOCK, (k + 1) * PERM_BLOCK)
            u = (h_ref[rows, :] * (1.0 + sc_ref[0]) + sh_ref[0]).astype(BF16)
            u_scr[rows, :] = _dot(perm, u).astype(BF16)
        raw = _dot(u_scr[...], wdt_ref[...]) + dtb_ref[...]
        sp = jnp.maximum(raw, 0.0) + jnp.log1p(jnp.exp(-jnp.abs(raw)))
        lane = lax.broadcasted_iota(jnp.int32, sp.shape, 1)
        dt_ref[...] = jnp.where(lane < n_dt, sp, 0.0)

    def conv_tile(with_gate):
        w = cw_ref[0]
        bias = cb_ref[0]
        row = lax.broadcasted_iota(jnp.int32, (SUBLANES, tile), 0)
        zeros = jnp.zeros((SUBLANES, tile), F32)
        blocks = [None] * nblk

        def up(v, nxt):
            return jnp.where(row == SUBLANES - 1, pltpu.roll(nxt, SUBLANES - 1, 0), pltpu.roll(v, SUBLANES - 1, 0))

        def down(v, prv):
            return jnp.where(row == 0, pltpu.roll(prv, 1, 0), pltpu.roll(v, 1, 0))

        def conv(k):
            x = blocks[k]
            nxt = blocks[k + 1] if (k + 1) % seq_blocks else None
            prv = blocks[k - 1] if k % seq_blocks else None
            u0 = up(x[0], zeros if nxt is None else nxt[0])
            u1 = up(x[1], zeros if nxt is None else nxt[1])
            d0 = down(x[nv - 2], zeros if prv is None else prv[nv - 2])
            d1 = down(x[nv - 1], zeros if prv is None else prv[nv - 1])
            shifted = (
                jnp.concatenate([d0[None], d1[None], x[:nv - 2]], axis=0),
                jnp.concatenate([d1[None], x[:nv - 1]], axis=0),
                x,
                jnp.concatenate([x[1:], u0[None]], axis=0),
                jnp.concatenate([x[2:], u0[None], u1[None]], axis=0),
            )
            acc = jnp.zeros((nv, SUBLANES, tile), F32) + bias
            for t in range(SSM_CONV_W):
                acc = acc + w[t:t + 1] * shifted[t]
            res = (acc * _sigmoid(acc)).astype(BF16)
            o_ref[0, k * PERM_BLOCK:(k + 1) * PERM_BLOCK, :] = res.reshape(PERM_BLOCK, tile)

        prows = min(PROJ_ROWS, rows_per_step)
        per_proj = prows // PERM_BLOCK

        def project(p):
            rows = slice(p * prows, (p + 1) * prows)
            r = _dot(u_scr[rows, :], w_ref[...])
            for i in range(per_proj):
                blocks[p * per_proj + i] = r[i * PERM_BLOCK:(i + 1) * PERM_BLOCK].reshape(nv, SUBLANES, tile)
            if with_gate:
                r = _dot(u_scr[rows, :], wg_ref[...])
                gate_ref[0, rows, :] = (r * _sigmoid(r)).astype(BF16)

        project(0)
        for p in range(rows_per_step // prows):
            if (p + 1) * prows < rows_per_step:
                project(p + 1)
            for i in range(per_proj):
                conv(p * per_proj + i)

    @pl.when(j < n_gate_tiles)
    def _():
        conv_tile(True)

    @pl.when(j >= n_gate_tiles)
    def _():
        conv_tile(False)


def _ssm_inproj_call(h, modt, w_main, w_dt, dt_bias, conv_w, conv_b, *, n_dt, n_gate_tiles, rows, seq, ctx_row, tile):
    t, d = h.shape
    assert rows % seq == 0 and t % rows == 0 and (ctx_row is not None or rows == seq)
    n_conv_tiles = w_main.shape[1] // tile - n_gate_tiles
    assert n_conv_tiles >= n_gate_tiles
    last_gate = n_gate_tiles - 1
    return pl.pallas_call(
        functools.partial(_ssm_inproj_kernel, n_dt=n_dt, n_gate_tiles=n_gate_tiles, seq=seq),
        grid=(t // rows, n_conv_tiles),
        in_specs=[
            pl.BlockSpec((rows, d), lambda b, j: (b, 0)),
            pl.BlockSpec((1, 1, d), _mod_index(0, rows, seq, ctx_row)),
            pl.BlockSpec((1, 1, d), _mod_index(1, rows, seq, ctx_row)),
            pl.BlockSpec((d, tile), lambda b, j: (0, jnp.minimum(j, last_gate))),
            pl.BlockSpec((d, tile), lambda b, j: (0, n_gate_tiles + j)),
            pl.BlockSpec((d, LANES), lambda b, j: (0, 0)),
            pl.BlockSpec((1, LANES), lambda b, j: (0, 0)),
            pl.BlockSpec((1, 8, tile), lambda b, j: (j, 0, 0)),
            pl.BlockSpec((1, 1, tile), lambda b, j: (j, 0, 0)),
        ],
        out_specs=[
            pl.BlockSpec((1, rows, tile), lambda b, j: (jnp.minimum(j, last_gate), b, 0)),
            pl.BlockSpec((1, rows, tile), lambda b, j: (j, b, 0)),
            pl.BlockSpec((rows, LANES), lambda b, j: (b, 0)),
        ],
        out_shape=[
            jax.ShapeDtypeStruct((n_gate_tiles, t, tile), BF16),
            jax.ShapeDtypeStruct((n_conv_tiles, t, tile), BF16),
            jax.ShapeDtypeStruct((t, LANES), F32),
        ],
        scratch_shapes=[pltpu.VMEM((rows, d), BF16)],
        compiler_params=_cparams(("arbitrary", "arbitrary")),
        name="ssm_inproj_conv",
    )(h, modt, modt, w_main, w_main, w_dt, dt_bias, conv_w, conv_b)


HEADS_PER_GROUP = 8
DT_LANES_PER_GROUP = 2 * HEADS_PER_GROUP


def _expand_heads(v, lane0, nrows):
    rows = max(nrows, 8)
    if rows != nrows:
        v = jnp.broadcast_to(v, (rows, LANES))
    lane = lax.broadcasted_iota(jnp.int32, (rows, LANES), 1)
    parts = []
    for p in range(HEADS_PER_GROUP // 2):
        idx = lane0 + 2 * p + (lane >> 6)
        parts.append(jnp.take_along_axis(v, idx, axis=1, mode="promise_in_bounds"))
    return jnp.concatenate(parts, axis=1)[:nrows]


def _cumsum_rows(a):
    q = a.shape[0]
    tril = (_token_of_row(lax.broadcasted_iota(jnp.int32, (q, q), 0))
            >= _token_of_row(lax.broadcasted_iota(jnp.int32, (q, q), 1)))
    tril = jnp.where(tril, 1.0, 0.0).astype(BF16)
    hi = a.astype(BF16)
    r1 = a - hi.astype(F32)
    mid = r1.astype(BF16)
    lo = (r1 - mid.astype(F32)).astype(BF16)
    return _dot(tril, hi) + _dot(tril, mid) + _dot(tril, lo)


def _chunk_terms(dt, a_neg):
    q = dt.shape[0]
    lane = lax.broadcasted_iota(jnp.int32, (q, LANES), 1)
    is_bwd = (lane & HEADS_PER_GROUP) != 0
    a = dt * a_neg
    acs = _cumsum_rows(a)
    tot = acs[q - 1:q, :]
    e = jnp.where(is_bwd, acs - a, acs)
    w_state = dt * jnp.exp(jnp.where(is_bwd, e, tot - acs))
    f_out = jnp.exp(jnp.where(is_bwd, tot - e, acs))
    dec = jnp.exp(tot)
    return e, w_state, f_out, dec


def _state_update(s_scr, idx, x, bm, w_state, dec, lane0):
    q = x.shape[0]
    xw = (x.astype(F32) * _expand_heads(w_state, lane0, q)).astype(BF16)
    s_scr[idx] = s_scr[idx] * _expand_heads(dec, lane0, 1) + _dot_tn(bm, xw)


def _ssd_kernel(alog_ref, d_ref, ng_ref, xcb_ref, dtcb_ref, xcf_ref, dtcf_ref, xb_ref, dtb_ref,
                xf_ref, zf_ref, dtf_ref, y_ref, sf_scr, sb_scr, sbs_scr, *, nb, ncc, ncl):
    o = pl.program_id(0)
    s = pl.program_id(1)
    par = o % 2
    prev = 1 - par
    entering = o < nb
    emitting = o >= 1
    lat = s >= ncc
    t = s - ncc
    a_neg = -jnp.exp(alog_ref[...])
    q = SSM_CHUNK
    n = SSM_STATE
    gcols = HEADS_PER_GROUP * SSM_HEAD_DIM

    def sweep_steps(s_scr, slot, x_ref, dt_ref, dir_off, keep=None):
        _, w_state, _, dec = _chunk_terms(dt_ref[...], a_neg)

        def one(g):
            idx = g if slot is None else (slot, g)
            if keep is not None:
                sbs_scr[par, keep, g] = s_scr[idx].astype(BF16)
            _state_update(s_scr, idx, x_ref[g], x_ref[SSM_GROUPS, :, g * n:(g + 1) * n], w_state, dec,
                          g * DT_LANES_PER_GROUP + dir_off)
        return [functools.partial(one, g) for g in range(SSM_GROUPS)]

    @pl.when((s == 0) & entering)
    def _():
        sf_scr[par] = jnp.zeros(sf_scr.shape[1:], F32)
        sb_scr[...] = jnp.zeros_like(sb_scr)

    @pl.when((s < ncc) & entering)
    def _():
        back = sweep_steps(sb_scr, None, xcb_ref, dtcb_ref, HEADS_PER_GROUP)
        fore = sweep_steps(sf_scr, par, xcf_ref, dtcf_ref, 0)
        for g in range(SSM_GROUPS):
            back[g]()
            fore[g]()

    def backward_steps():
        return sweep_steps(sb_scr, None, xb_ref, dtb_ref, HEADS_PER_GROUP, keep=ncl - 1 - t)

    def forward(between=None):
        dt = dtf_ref[...]
        e, w_state, f_out, dec = _chunk_terms(dt, a_neg)
        e_t = e.T
        dt_t = dt.T
        li = _token_of_row(lax.broadcasted_iota(jnp.int32, (q, q), 0))
        si = _token_of_row(lax.broadcasted_iota(jnp.int32, (q, q), 1))
        unperm = _perm_matrix(True)
        lower = li >= si
        below = li > si
        above = li < si
        sign = jnp.where(lower, 1.0, -1.0)
        dir_lane = jnp.where(lower, 0, HEADS_PER_GROUP)
        first = lax.broadcasted_iota(jnp.int32, (q, LANES), 1) < SSM_HEAD_DIM
        for g in range(SSM_GROUPS):
            lf = g * DT_LANES_PER_GROUP
            lb = lf + HEADS_PER_GROUP
            x = xf_ref[g]
            bm = xf_ref[SSM_GROUPS, :, g * n:(g + 1) * n]
            cm = xf_ref[SSM_GROUPS + 1, :, g * n:(g + 1) * n]
            cb = _dot_nt(cm, bm)
            y_f = _dot(cm, sf_scr[prev, g].astype(BF16))
            y_b = _dot(cm, sbs_scr[prev, t, g])
            y_parts = []
            for p in range(HEADS_PER_GROUP // 2):
                ms = []
                for r in (2 * p, 2 * p + 1):
                    e_col = jnp.take_along_axis(e, dir_lane + (lf + r), axis=1, mode="promise_in_bounds")
                    e_row = jnp.where(lower, e_t[lf + r:lf + r + 1, :], e_t[lb + r:lb + r + 1, :])
                    dt_f = dt_t[lf + r:lf + r + 1, :]
                    dt_b = dt_t[lb + r:lb + r + 1, :]
                    dsel = jnp.where(below, dt_f, jnp.where(above, dt_b, dt_f + dt_b))
                    ms.append((cb * jnp.exp((e_col - e_row) * sign) * dsel).astype(BF16))
                xp = x[:, p * LANES:(p + 1) * LANES]
                zero = jnp.zeros_like(xp)
                x_bd = jnp.concatenate([jnp.where(first, xp, zero), jnp.where(first, zero, xp)], axis=0)
                y_parts.append(_dot(jnp.concatenate(ms, axis=1), x_bd))
            cols = slice(g * gcols, (g + 1) * gcols)
            y = (jnp.concatenate(y_parts, axis=1)
                 + y_f * _expand_heads(f_out, lf, q) + y_b * _expand_heads(f_out, lb, q)
                 + x.astype(F32) * d_ref[:, cols])
            yy = y * zf_ref[g].astype(F32)
            ms_ = jnp.mean(yy * yy, axis=-1, keepdims=True)
            y_out = (yy * lax.rsqrt(ms_ + RMS_EPS) * ng_ref[:, cols]).astype(BF16)
            y_ref[:, cols] = _dot(unperm, y_out).astype(BF16)
            _state_update(sf_scr, (prev, g), x, bm, w_state, dec, lf)
            if between is not None:
                between[g]()

    @pl.when(lat & entering & emitting)
    def _():
        forward(between=backward_steps())

    @pl.when(lat & jnp.logical_not(emitting))
    def _():
        for step in backward_steps():
            step()

    @pl.when(lat & jnp.logical_not(entering))
    def _():
        forward()


def _ssd_call(a_log, d_row, norm_g, xbc_c, dt_c, gate_l, xbc_l, dt_l, *, bsz, n_ctx, seq):
    q = SSM_CHUNK
    ncc = n_ctx // q
    ncl = seq // q
    gcols = xbc_l.shape[2]
    d_inner = SSM_GROUPS * gcols
    nsteps = ncc + ncl
    n_state = SSM_STATE

    def entering(o):
        return jnp.minimum(o, bsz - 1)

    def emitting(o):
        return jnp.maximum(o - 1, 0)

    def bchunk(s):
        return ncl - 1 - jnp.maximum(s - ncc, 0)

    def fchunk(s):
        return jnp.maximum(s - ncc, 0)

    def cbrow(o, s):
        return entering(o) * ncc + jnp.maximum(ncc - 1 - s, 0)

    def cfrow(o, s):
        return entering(o) * ncc + jnp.minimum(s, ncc - 1)

    def brow(o, s):
        return entering(o) * ncl + bchunk(s)

    def frow(o, s):
        return emitting(o) * ncl + fchunk(s)

    def orow(o, s):
        return jnp.where(o == 0, 0, frow(o, s))

    assert SSM_GROUPS * n_state == gcols
    cspec = (xbc_l.shape[0], q, gcols)
    in_specs = [
        pl.BlockSpec((1, LANES), lambda o, s: (0, 0)),
        pl.BlockSpec((1, d_inner), lambda o, s: (0, 0)),
        pl.BlockSpec((1, d_inner), lambda o, s: (0, 0)),
        pl.BlockSpec(cspec, lambda o, s: (0, cbrow(o, s), 0)),
        pl.BlockSpec((q, LANES), lambda o, s: (cbrow(o, s), 0)),
        pl.BlockSpec(cspec, lambda o, s: (0, cfrow(o, s), 0)),
        pl.BlockSpec((q, LANES), lambda o, s: (cfrow(o, s), 0)),
        pl.BlockSpec(cspec, lambda o, s: (0, brow(o, s), 0)),
        pl.BlockSpec((q, LANES), lambda o, s: (brow(o, s), 0)),
        pl.BlockSpec(cspec, lambda o, s: (0, frow(o, s), 0)),
        pl.BlockSpec((SSM_GROUPS, q, gcols), lambda o, s: (0, frow(o, s), 0)),
        pl.BlockSpec((q, LANES), lambda o, s: (frow(o, s), 0)),
    ]
    return pl.pallas_call(
        functools.partial(_ssd_kernel, nb=bsz, ncc=ncc, ncl=ncl),
        grid=(bsz + 1, nsteps),
        in_specs=in_specs,
        out_specs=pl.BlockSpec((q, d_inner), lambda o, s: (orow(o, s), 0)),
        out_shape=jax.ShapeDtypeStruct((bsz * seq, d_inner), BF16),
        scratch_shapes=[
            pltpu.VMEM((2, SSM_GROUPS, n_state, gcols), F32),
            pltpu.VMEM((SSM_GROUPS, n_state, gcols), F32),
            pltpu.VMEM((2, ncl, SSM_GROUPS, n_state, gcols), BF16),
        ],
        compiler_params=_cparams(("arbitrary", "arbitrary")),
        name="ssd_scan",
    )(a_log, d_row, norm_g, xbc_c, dt_c, xbc_c, dt_c, xbc_l, dt_l, xbc_l, gate_l, dt_l)


def _rope_tables(seq):
    rows = seq // GRID_W
    row = jnp.repeat(jnp.arange(rows), GRID_W).astype(F32)
    col = jnp.tile(jnp.arange(GRID_W), rows).astype(F32)
    inv_freq = ROPE_BASE ** (-jnp.arange(0, ROPE_AXIS_DIM, 2, dtype=F32) / ROPE_AXIS_DIM)
    ang_r = row[:, None] * inv_freq[None, :]
    ang_c = col[:, None] * inv_freq[None, :]
    cos = jnp.concatenate([jnp.cos(ang_r)] * 2 + [jnp.cos(ang_c)] * 2, axis=-1)
    sin = jnp.concatenate([-jnp.sin(ang_r), jnp.sin(ang_r), -jnp.sin(ang_c), jnp.sin(ang_c)], axis=-1)
    reps = LANES // HEAD_DIM
    return jnp.tile(cos, (1, reps)), jnp.tile(sin, (1, reps))


def _dup_heads(w):
    d, n = w.shape
    w = w.reshape(d, n // HEAD_DIM, 1, HEAD_DIM)
    return jnp.broadcast_to(w, (d, n // HEAD_DIM, LANES // HEAD_DIM, HEAD_DIM)).reshape(d, -1)


def _attention_layer(h_lat, h_ctx, modt, w_in, w_out, sink, ln_g, ln_b, *, bsz, seq, n_ctx, ctx_row, alpha, tm):
    nq = ATT_HEADS * HEAD_DIM
    nk = ATT_KV_HEADS * HEAD_DIM
    wq, wk, wv = w_in[:, :nq], w_in[:, nq:nq + nk], w_in[:, nq + nk:]
    wk, wv = _dup_heads(wk), _dup_heads(wv)
    w_cat = jnp.concatenate([wq, wk, wv], axis=1).astype(BF16)
    nrot = nq + wk.shape[1]
    tables = _rope_tables(seq)
    qkv_l, rot_l = _att_inproj_call(h_lat, modt, w_cat, tables, seq=seq, ctx_row=None, nq=nq, nrot=nrot, tm=tm)
    (qkv_c,) = _att_inproj_call(h_ctx, modt, w_cat, None, seq=n_ctx, ctx_row=ctx_row, nq=nq, nrot=nrot,
                                tm=min(tm, h_ctx.shape[0]))
    o_l = _attn_call(sink, qkv_l, rot_l, qkv_c, bsz=bsz, seq=seq, n_ctx=n_ctx, window=True)
    o_c = _attn_call(sink, qkv_c, None, qkv_c, bsz=bsz, seq=n_ctx, n_ctx=n_ctx, window=False)
    w_o = w_out.astype(BF16)
    h_lat = _outproj_ln_call(o_l, w_o, h_lat, modt, ln_g, ln_b, alpha=alpha, seq=seq, ctx_row=None, tm=tm)
    h_ctx = _outproj_ln_call(o_c, w_o, h_ctx, modt, ln_g, ln_b, alpha=alpha, seq=n_ctx, ctx_row=ctx_row,
                             tm=min(tm, h_ctx.shape[0]))
    return h_lat, h_ctx


def _ssm_layer(h_lat, h_ctx, modt, w_in, conv_w, conv_b, dt_bias, a_log, d_skip, norm_g, w_out, ln_g, ln_b,
               *, bsz, seq, n_ctx, ctx_row, alpha, tm):
    d_inner = w_out.shape[0]
    heads = d_inner // SSM_HEAD_DIM
    gcols = d_inner // SSM_GROUPS
    hpg = heads // SSM_GROUPS
    conv_dim = d_inner + 2 * SSM_GROUPS * SSM_STATE
    n_main = d_inner + conv_dim
    assert hpg == HEADS_PER_GROUP and gcols == SSM_GROUPS * SSM_STATE
    w_main = w_in[:, :n_main].astype(BF16)

    def regroup(v):
        lead = v.shape[:-2]
        v = v.reshape(lead + (2, SSM_GROUPS, hpg))
        v = jnp.moveaxis(v, -3, -2).reshape(lead + (SSM_GROUPS, 2 * hpg))
        return v

    w_dt = regroup(w_in[:, n_main:].reshape(-1, 2, heads)).reshape(-1, 2 * heads)
    w_dt = jnp.pad(w_dt, ((0, 0), (0, LANES - 2 * heads))).astype(BF16)
    dtb = jnp.pad(regroup(dt_bias).reshape(1, 2 * heads), ((0, 0), (0, LANES - 2 * heads)))
    a_log_row = jnp.pad(regroup(a_log).reshape(1, 2 * heads), ((0, 0), (0, LANES - 2 * heads)))
    d_row = jnp.repeat(d_skip, SSM_HEAD_DIM).reshape(1, d_inner)
    ng = norm_g.reshape(1, d_inner)
    ntile_conv = conv_dim // gcols
    cw = jnp.pad(conv_w, ((0, 8 - SSM_CONV_W), (0, 0))).reshape(8, ntile_conv, gcols).transpose(1, 0, 2)
    cb = conv_b.reshape(ntile_conv, 1, gcols)

    n_gate = d_inner // gcols
    gate_l, xbc_l, dt_l = _ssm_inproj_call(h_lat, modt, w_main, w_dt, dtb, cw, cb, n_dt=2 * heads,
                                           n_gate_tiles=n_gate, rows=seq, seq=seq, ctx_row=None, tile=gcols)
    ctx_rows = n_ctx * max(1, min(bsz, tm // n_ctx))
    while (bsz * n_ctx) % ctx_rows:
        ctx_rows -= n_ctx
    _, xbc_c, dt_c = _ssm_inproj_call(h_ctx, modt, w_main, w_dt, dtb, cw, cb, n_dt=2 * heads,
                                      n_gate_tiles=n_gate, rows=ctx_rows, seq=n_ctx, ctx_row=ctx_row, tile=gcols)
    y = _ssd_call(a_log_row, d_row, ng, xbc_c, dt_c, gate_l, xbc_l, dt_l, bsz=bsz, n_ctx=n_ctx, seq=seq)
    return _outproj_ln_call(y, w_out.astype(BF16), h_lat, modt, ln_g, ln_b, alpha=alpha, seq=seq, ctx_row=None, tm=tm)


def kernel(x, c, ctx, c_ctx, w_mod, b_mod, ln_mix_g, ln_mix_b, ln_ff_g, ln_ff_b, att_w_in, att_w_out, att_sink,
           ssm_w_in, ssm_conv_w, ssm_conv_b, ssm_dt_bias, ssm_a_log, ssm_d, ssm_norm_g, ssm_w_out, ff_w1, ff_w2):
    bsz, seq, d = x.shape
    n_ctx = ctx.shape[1]
    depth = w_mod.shape[0]
    alpha = (2.0 * depth) ** 0.25
    tm = 1024 if seq % 1024 == 0 and (bsz * n_ctx) % 1024 == 0 else 512
    assert seq % tm == 0 and (bsz * n_ctx) % min(tm, bsz * n_ctx) == 0 and seq % PERM_BLOCK == 0
    ctx_row = bsz
    rows = -(-(bsz + 1) // MOD_ROWS_ALIGN) * MOD_ROWS_ALIGN
    cc = jnp.concatenate([c, c_ctx[None, :], jnp.zeros((rows - bsz - 1, d), F32)], axis=0)
    mod = _mod_call(cc, w_mod, b_mod)
    h_lat = x.reshape(bsz * seq, d)
    h_ctx = ctx.reshape(bsz * n_ctx, d)
    for i in range(depth):
        last = i == depth - 1
        j = i // N_MIXERS
        modt = mod[i].reshape(rows * N_MOD, 1, d)
        if i % N_MIXERS == 0:
            h_lat, h_ctx_mix = _attention_layer(h_lat, h_ctx, modt, att_w_in[j], att_w_out[j], att_sink[j],
                                                ln_mix_g[i], ln_mix_b[i], bsz=bsz, seq=seq, n_ctx=n_ctx,
                                                ctx_row=ctx_row, alpha=alpha, tm=tm)
        else:
            if not last:
                raise NotImplementedError("context outputs of an SSD layer are only needed when it is not last")
            h_lat = _ssm_layer(h_lat, h_ctx, modt, ssm_w_in[j], ssm_conv_w[j], ssm_conv_b[j], ssm_dt_bias[j],
                               ssm_a_log[j], ssm_d[j], ssm_norm_g[j], ssm_w_out[j], ln_mix_g[i], ln_mix_b[i],
                               bsz=bsz, seq=seq, n_ctx=n_ctx, ctx_row=ctx_row, alpha=alpha, tm=tm)
            h_ctx_mix = None
        w1 = ff_w1[i].astype(BF16)
        w2 = ff_w2[i].astype(BF16)
        h_lat = _mlp_ln_call(h_lat, modt, w1, w2, ln_ff_g[i], ln_ff_b[i], alpha=alpha, seq=seq, ctx_row=None,
                             tm=tm, tf=1024)
        if not last:
            h_ctx = _mlp_ln_call(h_ctx_mix, modt, w1, w2, ln_ff_g[i], ln_ff_b[i], alpha=alpha, seq=n_ctx,
                                 ctx_row=ctx_row, tm=min(tm, h_ctx_mix.shape[0]), tf=1024)
    return h_lat.reshape(bsz, seq, d)
```

```python
import functools

import jax
import jax.numpy as jnp
from jax import lax
from jax.experimental import pallas as pl
from jax.experimental.pallas import tpu as pltpu

F32 = jnp.float32
BF16 = jnp.bfloat16

GRID_W = 64
N_MIXERS = 2
ATT_HEADS = 16
ATT_KV_HEADS = 4
HEAD_DIM = 64
ATT_GROUP = ATT_HEADS // ATT_KV_HEADS
WINDOW = 128
BLOCK = 128
ROPE_BASE = 10000.0
ROPE_AXIS_DIM = HEAD_DIM // 2

SSM_HEAD_DIM = 64
SSM_GROUPS = 4
SSM_STATE = 128
SSM_CONV_W = 5
SSM_CHUNK = 128

LOG2E = 1.4426950408889634
Q_SCALE = HEAD_DIM ** -0.5 * LOG2E

N_MOD = 6
LN_EPS = 1e-5
RMS_EPS = 1e-5
NEG_INF = -1e30

LANES = 128
MOD_ROWS_ALIGN = 16
ROW_SPLIT = 4
VMEM_LIMIT = 48 * 1024 * 1024


def _cparams(sem):
    return pltpu.CompilerParams(dimension_semantics=sem, vmem_limit_bytes=VMEM_LIMIT)


def _sigmoid(x):
    return 1.0 / (1.0 + jnp.exp(-x))


def _dot(a, b):
    return jnp.dot(a, b, preferred_element_type=F32)


def _dot_nt(a, b):
    return lax.dot_general(a, b, (((1,), (1,)), ((), ())), preferred_element_type=F32)


def _dot_tn(a, b):
    return lax.dot_general(a, b, (((0,), (0,)), ((), ())), preferred_element_type=F32)


def _layer_norm_rows(v, g, b):
    mu = jnp.mean(v, axis=-1, keepdims=True)
    d = v - mu
    var = jnp.mean(d * d, axis=-1, keepdims=True)
    return d * lax.rsqrt(var + LN_EPS) * g + b


def _mod_kernel(c_ref, w_ref, b_ref, o_ref):
    c = c_ref[...]
    s = (c * _sigmoid(c)).astype(BF16)
    o_ref[0] = _dot(s, w_ref[0].astype(BF16)) + b_ref[0]


def _mod_call(cc, w_mod, b_mod):
    depth, d, n = w_mod.shape
    rows = cc.shape[0]
    tn = 1024
    return pl.pallas_call(
        _mod_kernel,
        grid=(depth, n // tn),
        in_specs=[
            pl.BlockSpec((rows, d), lambda l, j: (0, 0)),
            pl.BlockSpec((1, d, tn), lambda l, j: (l, 0, j)),
            pl.BlockSpec((1, 1, tn), lambda l, j: (l, 0, j)),
        ],
        out_specs=pl.BlockSpec((1, rows, tn), lambda l, j: (l, 0, j)),
        out_shape=jax.ShapeDtypeStruct((depth, rows, n), F32),
        compiler_params=_cparams(("arbitrary", "arbitrary")),
        name="adaln_mod",
    )(cc, w_mod, b_mod.reshape(depth, 1, n))


def _mod_index(which, tm, seq, ctx_row):
    if ctx_row is None:
        return lambda i, *_: ((i * tm // seq) * N_MOD + which, 0, 0)
    return lambda i, *_: (ctx_row * N_MOD + which, 0, 0)


def _att_inproj_kernel(*refs, rope, nq):
    if rope:
        x_ref, sh_ref, sc_ref, w_ref, cos_ref, sin_ref, qkv_ref, rot_ref = refs
    else:
        x_ref, sh_ref, sc_ref, w_ref, qkv_ref = refs
    u = (x_ref[...] * (1.0 + sc_ref[0]) + sh_ref[0]).astype(BF16)
    acc = _dot(u, w_ref[...])
    n = acc.shape[1]
    nrot = rot_ref.shape[1] if rope else 0
    if rope:
        cos = cos_ref[...]
        sin = sin_ref[...]
        lane = lax.broadcasted_iota(jnp.int32, cos.shape, 1)
        first_half = (lane & (ROPE_AXIS_DIM - 1)) < (ROPE_AXIS_DIM // 2)
    for cidx in range(n // LANES):
        chunk = acc[:, cidx * LANES:(cidx + 1) * LANES]
        if cidx * LANES < nq:
            chunk = chunk * Q_SCALE
        qkv_ref[:, cidx * LANES:(cidx + 1) * LANES] = chunk.astype(BF16)
        if rope and cidx * LANES < nrot:
            half = ROPE_AXIS_DIM // 2
            swapped = jnp.where(first_half, pltpu.roll(chunk, LANES - half, 1), pltpu.roll(chunk, half, 1))
            rot_ref[:, cidx * LANES:(cidx + 1) * LANES] = (chunk * cos + swapped * sin).astype(BF16)


def _att_inproj_call(h, modt, w, tables, *, seq, ctx_row, nq, nrot, tm):
    t, d = h.shape
    n = w.shape[1]
    rope = tables is not None
    in_specs = [
        pl.BlockSpec((tm, d), lambda i: (i, 0)),
        pl.BlockSpec((1, 1, d), _mod_index(0, tm, seq, ctx_row)),
        pl.BlockSpec((1, 1, d), _mod_index(1, tm, seq, ctx_row)),
        pl.BlockSpec((d, n), lambda i: (0, 0)),
    ]
    args = [h, modt, modt, w]
    out_specs = [pl.BlockSpec((tm, n), lambda i: (i, 0))]
    out_shape = [jax.ShapeDtypeStruct((t, n), BF16)]
    if rope:
        per_seq = seq // tm
        in_specs += [pl.BlockSpec((tm, LANES), lambda i: (i % per_seq, 0))] * 2
        args += list(tables)
        out_specs.append(pl.BlockSpec((tm, nrot), lambda i: (i, 0)))
        out_shape.append(jax.ShapeDtypeStruct((t, nrot), BF16))
    return pl.pallas_call(
        functools.partial(_att_inproj_kernel, rope=rope, nq=nq),
        grid=(t // tm,),
        in_specs=in_specs,
        out_specs=out_specs,
        out_shape=out_shape,
        compiler_params=_cparams(("arbitrary",)),
        name="att_inproj_rope" if rope else "att_inproj",
    )(*args)


def _attn_kernel(*refs, window, nb):
    if window:
        (sink_ref, qr_ref, qp_ref, kp_ref, kc_ref, kn_ref, vp_ref, vc_ref, vn_ref,
         kx_ref, vx_ref, o_ref) = refs
    else:
        sink_ref, qp_ref, kx_ref, vx_ref, o_ref = refs
    i = pl.program_id(1)
    rows = ATT_GROUP * BLOCK
    half_lane = lax.broadcasted_iota(jnp.int32, (BLOCK, LANES), 1) < HEAD_DIM
    ridx = lax.broadcasted_iota(jnp.int32, (rows, 1), 0)
    if window:
        li = lax.broadcasted_iota(jnp.int32, (BLOCK, BLOCK), 0)
        ci = lax.broadcasted_iota(jnp.int32, (BLOCK, BLOCK), 1)
        ok_prev = ci >= li + jnp.where(i > 0, 0, BLOCK)
        ok_next = ci + jnp.where(i < nb - 1, 0, BLOCK) <= li

    def masked(s_blk, ok):
        return jnp.concatenate([jnp.where(ok, s_blk[r * BLOCK:(r + 1) * BLOCK], NEG_INF)
                                for r in range(ATT_GROUP)], axis=0)

    def with_ones(v):
        return jnp.concatenate([v, jnp.ones_like(v)], axis=1)

    def stack_heads(q_ref, g):
        parts = []
        for p in range(ATT_GROUP // 2):
            c0 = (g * (ATT_GROUP // 2) + p) * LANES
            pair = q_ref[0, :, c0:c0 + LANES]
            zero = jnp.zeros_like(pair)
            parts.append(jnp.where(half_lane, pair, zero))
            parts.append(jnp.where(half_lane, zero, pair))
        return jnp.concatenate(parts, axis=0)

    def scores(g):
        gl = slice(g * LANES, (g + 1) * LANES)
        s_ctx = _dot_nt(stack_heads(qp_ref, g), kx_ref[0, :, gl])
        if not window:
            return s_ctx, None
        kw = jnp.concatenate([kp_ref[0, :, gl], kc_ref[0, :, gl], kn_ref[0, :, gl]], axis=0)
        s_win = _dot_nt(stack_heads(qr_ref, g), kw)
        s_win = jnp.concatenate([masked(s_win[:, :BLOCK], ok_prev), s_win[:, BLOCK:2 * BLOCK],
                                 masked(s_win[:, 2 * BLOCK:], ok_next)], axis=1)
        return s_ctx, s_win

    ahead = scores(0)
    for g in range(ATT_KV_HEADS):
        s_ctx, s_win = ahead
        if g + 1 < ATT_KV_HEADS:
            ahead = scores(g + 1)
        gl = slice(g * LANES, (g + 1) * LANES)
        sink_col = jnp.full((rows, 1), sink_ref[g * ATT_GROUP + ATT_GROUP - 1] * LOG2E, F32)
        for r in range(ATT_GROUP - 2, -1, -1):
            sink_col = jnp.where(ridx < (r + 1) * BLOCK, sink_ref[g * ATT_GROUP + r] * LOG2E, sink_col)
        pieces = [s_ctx[:, k * LANES:(k + 1) * LANES] for k in range(s_ctx.shape[1] // LANES)]
        if window:
            pieces += [s_win[:, k * BLOCK:(k + 1) * BLOCK] for k in range(3)]
        mx = pieces[0]
        for piece in pieces[1:]:
            mx = jnp.maximum(mx, piece)
        m = jnp.maximum(jnp.max(mx, axis=-1, keepdims=True), sink_col)
        p_ctx = jnp.exp2(s_ctx - m)
        if window:
            vw = jnp.concatenate([vp_ref[0, :, gl], vc_ref[0, :, gl], vn_ref[0, :, gl]], axis=0)
            acc = (_dot(p_ctx.astype(BF16), with_ones(vx_ref[0, :, gl]))
                   + _dot(jnp.exp2(s_win - m).astype(BF16), with_ones(vw)))
            denom = acc[:, LANES:LANES + 1] + jnp.exp2(sink_col - m)
            o = acc[:, :LANES] * (1.0 / denom)
        else:
            denom = jnp.sum(p_ctx, axis=-1, keepdims=True) + jnp.exp2(sink_col - m)
            o = _dot(p_ctx.astype(BF16), vx_ref[0, :, gl]) * (1.0 / denom)
        for p in range(ATT_GROUP // 2):
            a = o[(2 * p) * BLOCK:(2 * p + 1) * BLOCK]
            b = o[(2 * p + 1) * BLOCK:(2 * p + 2) * BLOCK]
            c0 = (g * (ATT_GROUP // 2) + p) * LANES
            o_ref[0, :, c0:c0 + LANES] = jnp.where(half_lane, a, b).astype(BF16)


def _attn_call(sink, qkv, rot, qkv_ctx, *, bsz, seq, n_ctx, window):
    nq = ATT_HEADS * HEAD_DIM
    nkv = ATT_KV_HEADS * LANES
    nb = seq // BLOCK
    qkv3 = qkv.reshape(bsz, seq, qkv.shape[1])
    ctx3 = qkv_ctx.reshape(bsz, n_ctx, qkv_ctx.shape[1])
    kcol = nq // nkv
    vcol = kcol + 1
    smem = pl.BlockSpec(memory_space=pltpu.SMEM)
    qspec = pl.BlockSpec((1, BLOCK, nq), lambda b, i: (b, i, 0))
    kx = pl.BlockSpec((1, n_ctx, nkv), lambda b, i: (b, 0, kcol))
    vx = pl.BlockSpec((1, n_ctx, nkv), lambda b, i: (b, 0, vcol))
    if window:
        rot3 = rot.reshape(bsz, seq, rot.shape[1])

        def kv_specs(colblk):
            return [
                pl.BlockSpec((1, BLOCK, nkv), lambda b, i: (b, jnp.maximum(i - 1, 0), colblk)),
                pl.BlockSpec((1, BLOCK, nkv), lambda b, i: (b, i, colblk)),
                pl.BlockSpec((1, BLOCK, nkv), lambda b, i: (b, jnp.minimum(i + 1, nb - 1), colblk)),
            ]
        in_specs = [smem, qspec, qspec] + kv_specs(kcol) + kv_specs(vcol) + [kx, vx]
        args = [sink, rot3, qkv3, rot3, rot3, rot3, qkv3, qkv3, qkv3, ctx3, ctx3]
    else:
        in_specs = [smem, qspec, kx, vx]
        args = [sink, qkv3, ctx3, ctx3]
    out = pl.pallas_call(
        functools.partial(_attn_kernel, window=window, nb=nb),
        grid=(bsz, nb),
        in_specs=in_specs,
        out_specs=pl.BlockSpec((1, BLOCK, nq), lambda b, i: (b, i, 0)),
        out_shape=jax.ShapeDtypeStruct((bsz, seq, nq), BF16),
        compiler_params=_cparams(("arbitrary", "arbitrary")),
        name="attn_window" if window else "attn_ctx",
    )(*args)
    return out.reshape(bsz * seq, nq)


def _outproj_ln_kernel(a_ref, w_ref, h_ref, gate_ref, g_ref, b_ref, o_ref, *, alpha):
    sub = a_ref.shape[0] // ROW_SPLIT
    for k in range(ROW_SPLIT):
        rows = slice(k * sub, (k + 1) * sub)
        y = _dot(a_ref[rows, :], w_ref[...])
        v = alpha * h_ref[rows, :] + gate_ref[0] * y
        o_ref[rows, :] = _layer_norm_rows(v, g_ref[...], b_ref[...])


def _outproj_ln_call(a, w, h, modt, ln_g, ln_b, *, alpha, seq, ctx_row, tm):
    t, k = a.shape
    d = w.shape[1]
    return pl.pallas_call(
        functools.partial(_outproj_ln_kernel, alpha=alpha),
        grid=(t // tm,),
        in_specs=[
            pl.BlockSpec((tm, k), lambda i: (i, 0)),
            pl.BlockSpec((k, d), lambda i: (0, 0)),
            pl.BlockSpec((tm, d), lambda i: (i, 0)),
            pl.BlockSpec((1, 1, d), _mod_index(2, tm, seq, ctx_row)),
            pl.BlockSpec((1, d), lambda i: (0, 0)),
            pl.BlockSpec((1, d), lambda i: (0, 0)),
        ],
        out_specs=pl.BlockSpec((tm, d), lambda i: (i, 0)),
        out_shape=jax.ShapeDtypeStruct((t, d), F32),
        compiler_params=_cparams(("arbitrary",)),
        name="outproj_ln",
    )(a, w, h, modt, ln_g.reshape(1, d), ln_b.reshape(1, d))


def _mlp_ln_kernel(h_ref, sh_ref, sc_ref, gate_ref, w1_ref, w2_ref, g_ref, b_ref, o_ref, u_scr, acc_scr, *, alpha):
    j = pl.program_id(1)

    @pl.when(j == 0)
    def _():
        u_scr[...] = (h_ref[...] * (1.0 + sc_ref[0]) + sh_ref[0]).astype(BF16)
        acc_scr[...] = jnp.zeros_like(acc_scr)

    last = pl.num_programs(1) - 1

    def hidden(rows):
        a = jnp.maximum(_dot(u_scr[rows, :], w1_ref[...]), 0.0)
        return _dot((a * a).astype(BF16), w2_ref[...])

    @pl.when(j < last)
    def _():
        acc_scr[...] += hidden(slice(None))

    @pl.when(j == last)
    def _():
        sub = h_ref.shape[0] // ROW_SPLIT
        for k in range(ROW_SPLIT):
            rows = slice(k * sub, (k + 1) * sub)
            v = alpha * h_ref[rows, :] + gate_ref[0] * (acc_scr[rows, :] + hidden(rows))
            o_ref[rows, :] = _layer_norm_rows(v, g_ref[...], b_ref[...])


def _mlp_ln_call(h, modt, w1, w2, ln_g, ln_b, *, alpha, seq, ctx_row, tm, tf):
    t, d = h.shape
    ff = w1.shape[1]
    return pl.pallas_call(
        functools.partial(_mlp_ln_kernel, alpha=alpha),
        grid=(t // tm, ff // tf),
        in_specs=[
            pl.BlockSpec((tm, d), lambda i, j: (i, 0)),
            pl.BlockSpec((1, 1, d), _mod_index(3, tm, seq, ctx_row)),
            pl.BlockSpec((1, 1, d), _mod_index(4, tm, seq, ctx_row)),
            pl.BlockSpec((1, 1, d), _mod_index(5, tm, seq, ctx_row)),
            pl.BlockSpec((d, tf), lambda i, j: (0, j)),
            pl.BlockSpec((tf, d), lambda i, j: (j, 0)),
            pl.BlockSpec((1, d), lambda i, j: (0, 0)),
            pl.BlockSpec((1, d), lambda i, j: (0, 0)),
        ],
        out_specs=pl.BlockSpec((tm, d), lambda i, j: (i, 0)),
        out_shape=jax.ShapeDtypeStruct((t, d), F32),
        scratch_shapes=[pltpu.VMEM((tm, d), BF16), pltpu.VMEM((tm, d), F32)],
        compiler_params=_cparams(("arbitrary", "arbitrary")),
        name="mlp_ln",
    )(h, modt, modt, modt, w1, w2, ln_g.reshape(1, d), ln_b.reshape(1, d))


SUBLANES = 8
PERM_BLOCK = SSM_CHUNK
PERM_VREGS = PERM_BLOCK // SUBLANES
PROJ_ROWS = PERM_BLOCK


def _token_of_row(r):
    return (r & (SUBLANES - 1)) * PERM_VREGS + (r >> 3)


def _perm_matrix(inverse):
    rr = lax.broadcasted_iota(jnp.int32, (PERM_BLOCK, PERM_BLOCK), 0)
    cc = lax.broadcasted_iota(jnp.int32, (PERM_BLOCK, PERM_BLOCK), 1)
    hit = (rr == _token_of_row(cc)) if inverse else (cc == _token_of_row(rr))
    return jnp.where(hit, 1.0, 0.0).astype(BF16)


def _ssm_inproj_kernel(h_ref, sh_ref, sc_ref, wg_ref, w_ref, wdt_ref, dtb_ref, cw_ref, cb_ref,
                       gate_ref, o_ref, dt_ref, u_scr, *, n_dt, n_gate_tiles, seq):
    j = pl.program_id(1)
    rows_per_step = h_ref.shape[0]
    tile = o_ref.shape[2]
    nblk = rows_per_step // PERM_BLOCK
    seq_blocks = seq // PERM_BLOCK
    nv = PERM_VREGS

    @pl.when(j == 0)
    def _():
        perm = _perm_matrix(False)
        for k in range(nblk):
            rows = slice(k * PERM_BLOCK, (k + 1) * PERM_BLOCK)
            u = (h_ref[rows, :] * (1.0 + sc_ref[0]) + sh_ref[0]).astype(BF16)
            u_scr[rows, :] = _dot(perm, u).astype(BF16)
        raw = _dot(u_scr[...], wdt_ref[...]) + dtb_ref[...]
        sp = jnp.maximum(raw, 0.0) + jnp.log1p(jnp.exp(-jnp.abs(raw)))
        lane = lax.broadcasted_iota(jnp.int32, sp.shape, 1)
        dt_ref[...] = jnp.where(lane < n_dt, sp, 0.0)

    def conv_tile(with_gate):
        w = cw_ref[0]
        bias = cb_ref[0]
        row = lax.broadcasted_iota(jnp.int32, (SUBLANES, tile), 0)
        zeros = jnp.zeros((SUBLANES, tile), F32)
        blocks = [None] * nblk

        def up(v, nxt):
            return jnp.where(row == SUBLANES - 1, pltpu.roll(nxt, SUBLANES - 1, 0), pltpu.roll(v, SUBLANES - 1, 0))

        def down(v, prv):
            return jnp.where(row == 0, pltpu.roll(prv, 1, 0), pltpu.roll(v, 1, 0))

        def conv(k):
            x = blocks[k]
            nxt = blocks[k + 1] if (k + 1) % seq_blocks else None
            prv = blocks[k - 1] if k % seq_blocks else None
            u0 = up(x[0], zeros if nxt is None else nxt[0])
            u1 = up(x[1], zeros if nxt is None else nxt[1])
            d0 = down(x[nv - 2], zeros if prv is None else prv[nv - 2])
            d1 = down(x[nv - 1], zeros if prv is None else prv[nv - 1])
            shifted = (
                jnp.concatenate([d0[None], d1[None], x[:nv - 2]], axis=0),
                jnp.concatenate([d1[None], x[:nv - 1]], axis=0),
                x,
                jnp.concatenate([x[1:], u0[None]], axis=0),
                jnp.concatenate([x[2:], u0[None], u1[None]], axis=0),
            )
            acc = jnp.zeros((nv, SUBLANES, tile), F32) + bias
            for t in range(SSM_CONV_W):
                acc = acc + w[t:t + 1] * shifted[t]
            res = (acc * _sigmoid(acc)).astype(BF16)
            o_ref[0, k * PERM_BLOCK:(k + 1) * PERM_BLOCK, :] = res.reshape(PERM_BLOCK, tile)

        prows = min(PROJ_ROWS, rows_per_step)
        per_proj = prows // PERM_BLOCK

        def project(p):
            rows = slice(p * prows, (p + 1) * prows)
            r = _dot(u_scr[rows, :], w_ref[...])
            for i in range(per_proj):
                blocks[p * per_proj + i] = r[i * PERM_BLOCK:(i + 1) * PERM_BLOCK].reshape(nv, SUBLANES, tile)
            if with_gate:
                r = _dot(u_scr[rows, :], wg_ref[...])
                gate_ref[0, rows, :] = (r * _sigmoid(r)).astype(BF16)

        project(0)
        for p in range(rows_per_step // prows):
            if (p + 1) * prows < rows_per_step:
                project(p + 1)
            for i in range(per_proj):
                conv(p * per_proj + i)

    @pl.when(j < n_gate_tiles)
    def _():
        conv_tile(True)

    @pl.when(j >= n_gate_tiles)
    def _():
        conv_tile(False)


def _ssm_inproj_call(h, modt, w_main, w_dt, dt_bias, conv_w, conv_b, *, n_dt, n_gate_tiles, rows, seq, ctx_row, tile):
    t, d = h.shape
    assert rows % seq == 0 and t % rows == 0 and (ctx_row is not None or rows == seq)
    n_conv_tiles = w_main.shape[1] // tile - n_gate_tiles
    assert n_conv_tiles >= n_gate_tiles
    last_gate = n_gate_tiles - 1
    return pl.pallas_call(
        functools.partial(_ssm_inproj_kernel, n_dt=n_dt, n_gate_tiles=n_gate_tiles, seq=seq),
        grid=(t // rows, n_conv_tiles),
        in_specs=[
            pl.BlockSpec((rows, d), lambda b, j: (b, 0)),
            pl.BlockSpec((1, 1, d), _mod_index(0, rows, seq, ctx_row)),
            pl.BlockSpec((1, 1, d), _mod_index(1, rows, seq, ctx_row)),
            pl.BlockSpec((d, tile), lambda b, j: (0, jnp.minimum(j, last_gate))),
            pl.BlockSpec((d, tile), lambda b, j: (0, n_gate_tiles + j)),
            pl.BlockSpec((d, LANES), lambda b, j: (0, 0)),
            pl.BlockSpec((1, LANES), lambda b, j: (0, 0)),
            pl.BlockSpec((1, 8, tile), lambda b, j: (j, 0, 0)),
            pl.BlockSpec((1, 1, tile), lambda b, j: (j, 0, 0)),
        ],
        out_specs=[
            pl.BlockSpec((1, rows, tile), lambda b, j: (jnp.minimum(j, last_gate), b, 0)),
            pl.BlockSpec((1, rows, tile), lambda b, j: (j, b, 0)),
            pl.BlockSpec((rows, LANES), lambda b, j: (b, 0)),
        ],
        out_shape=[
            jax.ShapeDtypeStruct((n_gate_tiles, t, tile), BF16),
            jax.ShapeDtypeStruct((n_conv_tiles, t, tile), BF16),
            jax.ShapeDtypeStruct((t, LANES), F32),
        ],
        scratch_shapes=[pltpu.VMEM((rows, d), BF16)],
        compiler_params=_cparams(("arbitrary", "arbitrary")),
        name="ssm_inproj_conv",
    )(h, modt, modt, w_main, w_main, w_dt, dt_bias, conv_w, conv_b)


HEADS_PER_GROUP = 8
DT_LANES_PER_GROUP = 2 * HEADS_PER_GROUP


def _expand_heads(v, lane0, nrows):
    rows = max(nrows, 8)
    if rows != nrows:
        v = jnp.broadcast_to(v, (rows, LANES))
    lane = lax.broadcasted_iota(jnp.int32, (rows, LANES), 1)
    parts = []
    for p in range(HEADS_PER_GROUP // 2):
        idx = lane0 + 2 * p + (lane >> 6)
        parts.append(jnp.take_along_axis(v, idx, axis=1, mode="promise_in_bounds"))
    return jnp.concatenate(parts, axis=1)[:nrows]


def _cumsum_rows(a):
    q = a.shape[0]
    tril = (_token_of_row(lax.broadcasted_iota(jnp.int32, (q, q), 0))
            >= _token_of_row(lax.broadcasted_iota(jnp.int32, (q, q), 1)))
    tril = jnp.where(tril, 1.0, 0.0).astype(BF16)
    hi = a.astype(BF16)
    r1 = a - hi.astype(F32)
    mid = r1.astype(BF16)
    lo = (r1 - mid.astype(F32)).astype(BF16)
    parts = _dot(tril, jnp.concatenate([hi, mid, lo], axis=1))
    return parts[:, :LANES] + parts[:, LANES:2 * LANES] + parts[:, 2 * LANES:]


def _chunk_terms(dt, a_neg):
    q = dt.shape[0]
    lane = lax.broadcasted_iota(jnp.int32, (q, LANES), 1)
    is_bwd = (lane & HEADS_PER_GROUP) != 0
    a = dt * a_neg
    acs = _cumsum_rows(a)
    tot = acs[q - 1:q, :]
    e = jnp.where(is_bwd, acs - a, acs)
    w_state = dt * jnp.exp(jnp.where(is_bwd, e, tot - acs))
    f_out = jnp.exp(jnp.where(is_bwd, tot - e, acs))
    dec = jnp.exp(tot)
    return e, w_state, f_out, dec


def _state_update(s_scr, idx, x, bm, w_state, dec, lane0):
    q = x.shape[0]
    xw = (x.astype(F32) * _expand_heads(w_state, lane0, q)).astype(BF16)
    s_scr[idx] = s_scr[idx] * _expand_heads(dec, lane0, 1) + _dot_tn(bm, xw)


def _ssd_kernel(alog_ref, d_ref, ng_ref, xcb_ref, dtcb_ref, xcf_ref, dtcf_ref, xb_ref, dtb_ref,
                xf_ref, zf_ref, dtf_ref, y_ref, sf_scr, sb_scr, sbs_scr, *, nb, ncc, ncl):
    o = pl.program_id(0)
    s = pl.program_id(1)
    par = o % 2
    prev = 1 - par
    entering = o < nb
    emitting = o >= 1
    lat = s >= ncc
    t = s - ncc
    a_neg = -jnp.exp(alog_ref[...])
    q = SSM_CHUNK
    n = SSM_STATE
    gcols = HEADS_PER_GROUP * SSM_HEAD_DIM

    def sweep_steps(s_scr, slot, x_ref, dt_ref, dir_off, keep=None):
        _, w_state, _, dec = _chunk_terms(dt_ref[...], a_neg)

        def one(g):
            idx = g if slot is None else (slot, g)
            if keep is not None:
                sbs_scr[par, keep, g] = s_scr[idx].astype(BF16)
            _state_update(s_scr, idx, x_ref[g], x_ref[SSM_GROUPS, :, g * n:(g + 1) * n], w_state, dec,
                          g * DT_LANES_PER_GROUP + dir_off)
        return [functools.partial(one, g) for g in range(SSM_GROUPS)]

    @pl.when((s == 0) & entering)
    def _():
        sf_scr[par] = jnp.zeros(sf_scr.shape[1:], F32)
        sb_scr[...] = jnp.zeros_like(sb_scr)

    @pl.when((s < ncc) & entering)
    def _():
        back = sweep_steps(sb_scr, None, xcb_ref, dtcb_ref, HEADS_PER_GROUP)
        fore = sweep_steps(sf_scr, par, xcf_ref, dtcf_ref, 0)
        for g in range(SSM_GROUPS):
            back[g]()
            fore[g]()

    def backward_steps():
        return sweep_steps(sb_scr, None, xb_ref, dtb_ref, HEADS_PER_GROUP, keep=ncl - 1 - t)

    def forward(between=None):
        dt = dtf_ref[...]
        e, w_state, f_out, dec = _chunk_terms(dt, a_neg)
        e_t = e.T
        dt_t = dt.T
        li = _token_of_row(lax.broadcasted_iota(jnp.int32, (q, q), 0))
        si = _token_of_row(lax.broadcasted_iota(jnp.int32, (q, q), 1))
        unperm = _perm_matrix(True)
        lower = li >= si
        below = li > si
        above = li < si
        sign = jnp.where(lower, 1.0, -1.0)
        dir_lane = jnp.where(lower, 0, HEADS_PER_GROUP)
        first = lax.broadcasted_iota(jnp.int32, (q, LANES), 1) < SSM_HEAD_DIM
        y_outs = []
        for g in range(SSM_GROUPS):
            lf = g * DT_LANES_PER_GROUP
            lb = lf + HEADS_PER_GROUP
            x = xf_ref[g]
            bm = xf_ref[SSM_GROUPS, :, g * n:(g + 1) * n]
            cm = xf_ref[SSM_GROUPS + 1, :, g * n:(g + 1) * n]
            cb = _dot_nt(cm, bm)
            states = jnp.concatenate([sf_scr[prev, g].astype(BF16), sbs_scr[prev, t, g]], axis=1)
            y_fb = _dot(cm, states)
            y_f = y_fb[:, :gcols]
            y_b = y_fb[:, gcols:]
            y_parts = []
            for p in range(HEADS_PER_GROUP // 2):
                ms = []
                for r in (2 * p, 2 * p + 1):
                    e_col = jnp.take_along_axis(e, dir_lane + (lf + r), axis=1, mode="promise_in_bounds")
                    e_row = jnp.where(lower, e_t[lf + r:lf + r + 1, :], e_t[lb + r:lb + r + 1, :])
                    dt_f = dt_t[lf + r:lf + r + 1, :]
                    dt_b = dt_t[lb + r:lb + r + 1, :]
                    dsel = jnp.where(below, dt_f, jnp.where(above, dt_b, dt_f + dt_b))
                    ms.append((cb * jnp.exp((e_col - e_row) * sign) * dsel).astype(BF16))
                xp = x[:, p * LANES:(p + 1) * LANES]
                zero = jnp.zeros_like(xp)
                x_bd = jnp.concatenate([jnp.where(first, xp, zero), jnp.where(first, zero, xp)], axis=0)
                y_parts.append(_dot(jnp.concatenate(ms, axis=1), x_bd))
            cols = slice(g * gcols, (g + 1) * gcols)
            y = (jnp.concatenate(y_parts, axis=1)
                 + y_f * _expand_heads(f_out, lf, q) + y_b * _expand_heads(f_out, lb, q)
                 + x.astype(F32) * d_ref[:, cols])
            yy = y * zf_ref[g].astype(F32)
            ms_ = jnp.mean(yy * yy, axis=-1, keepdims=True)
            y_outs.append((yy * lax.rsqrt(ms_ + RMS_EPS) * ng_ref[:, cols]).astype(BF16))
            _state_update(sf_scr, (prev, g), x, bm, w_state, dec, lf)
            if between is not None:
                between[g]()
        y_ref[...] = _dot(unperm, jnp.concatenate(y_outs, axis=1)).astype(BF16)

    @pl.when(lat & entering & emitting)
    def _():
        forward(between=backward_steps())

    @pl.when(lat & jnp.logical_not(emitting))
    def _():
        for step in backward_steps():
            step()

    @pl.when(lat & jnp.logical_not(entering))
    def _():
        forward()


def _ssd_call(a_log, d_row, norm_g, xbc_c, dt_c, gate_l, xbc_l, dt_l, *, bsz, n_ctx, seq):
    q = SSM_CHUNK
    ncc = n_ctx // q
    ncl = seq // q
    gcols = xbc_l.shape[2]
    d_inner = SSM_GROUPS * gcols
    nsteps = ncc + ncl
    n_state = SSM_STATE

    def entering(o):
        return jnp.minimum(o, bsz - 1)

    def emitting(o):
        return jnp.maximum(o - 1, 0)

    def bchunk(s):
        return ncl - 1 - jnp.maximum(s - ncc, 0)

    def fchunk(s):
        return jnp.maximum(s - ncc, 0)

    def cbrow(o, s):
        return entering(o) * ncc + jnp.maximum(ncc - 1 - s, 0)

    def cfrow(o, s):
        return entering(o) * ncc + jnp.minimum(s, ncc - 1)

    def brow(o, s):
        return entering(o) * ncl + bchunk(s)

    def frow(o, s):
        return emitting(o) * ncl + fchunk(s)

    def orow(o, s):
        return jnp.where(o == 0, 0, frow(o, s))

    assert SSM_GROUPS * n_state == gcols
    cspec = (xbc_l.shape[0], q, gcols)
    in_specs = [
        pl.BlockSpec((1, LANES), lambda o, s: (0, 0)),
        pl.BlockSpec((1, d_inner), lambda o, s: (0, 0)),
        pl.BlockSpec((1, d_inner), lambda o, s: (0, 0)),
        pl.BlockSpec(cspec, lambda o, s: (0, cbrow(o, s), 0)),
        pl.BlockSpec((q, LANES), lambda o, s: (cbrow(o, s), 0)),
        pl.BlockSpec(cspec, lambda o, s: (0, cfrow(o, s), 0)),
        pl.BlockSpec((q, LANES), lambda o, s: (cfrow(o, s), 0)),
        pl.BlockSpec(cspec, lambda o, s: (0, brow(o, s), 0)),
        pl.BlockSpec((q, LANES), lambda o, s: (brow(o, s), 0)),
        pl.BlockSpec(cspec, lambda o, s: (0, frow(o, s), 0)),
        pl.BlockSpec((SSM_GROUPS, q, gcols), lambda o, s: (0, frow(o, s), 0)),
        pl.BlockSpec((q, LANES), lambda o, s: (frow(o, s), 0)),
    ]
    return pl.pallas_call(
        functools.partial(_ssd_kernel, nb=bsz, ncc=ncc, ncl=ncl),
        grid=(bsz + 1, nsteps),
        in_specs=in_specs,
        out_specs=pl.BlockSpec((q, d_inner), lambda o, s: (orow(o, s), 0)),
        out_shape=jax.ShapeDtypeStruct((bsz * seq, d_inner), BF16),
        scratch_shapes=[
            pltpu.VMEM((2, SSM_GROUPS, n_state, gcols), F32),
            pltpu.VMEM((SSM_GROUPS, n_state, gcols), F32),
            pltpu.VMEM((2, ncl, SSM_GROUPS, n_state, gcols), BF16),
        ],
        compiler_params=_cparams(("arbitrary", "arbitrary")),
        name="ssd_scan",
    )(a_log, d_row, norm_g, xbc_c, dt_c, xbc_c, dt_c, xbc_l, dt_l, xbc_l, gate_l, dt_l)


def _rope_tables(seq):
    rows = seq // GRID_W
    row = jnp.repeat(jnp.arange(rows), GRID_W).astype(F32)
    col = jnp.tile(jnp.arange(GRID_W), rows).astype(F32)
    inv_freq = ROPE_BASE ** (-jnp.arange(0, ROPE_AXIS_DIM, 2, dtype=F32) / ROPE_AXIS_DIM)
    ang_r = row[:, None] * inv_freq[None, :]
    ang_c = col[:, None] * inv_freq[None, :]
    cos = jnp.concatenate([jnp.cos(ang_r)] * 2 + [jnp.cos(ang_c)] * 2, axis=-1)
    sin = jnp.concatenate([-jnp.sin(ang_r), jnp.sin(ang_r), -jnp.sin(ang_c), jnp.sin(ang_c)], axis=-1)
    reps = LANES // HEAD_DIM
    return jnp.tile(cos, (1, reps)), jnp.tile(sin, (1, reps))


def _dup_heads(w):
    d, n = w.shape
    w = w.reshape(d, n // HEAD_DIM, 1, HEAD_DIM)
    return jnp.broadcast_to(w, (d, n // HEAD_DIM, LANES // HEAD_DIM, HEAD_DIM)).reshape(d, -1)


def _attention_layer(h_lat, h_ctx, modt, w_in, w_out, sink, ln_g, ln_b, *, bsz, seq, n_ctx, ctx_row, alpha, tm):
    nq = ATT_HEADS * HEAD_DIM
    nk = ATT_KV_HEADS * HEAD_DIM
    wq, wk, wv = w_in[:, :nq], w_in[:, nq:nq + nk], w_in[:, nq + nk:]
    wk, wv = _dup_heads(wk), _dup_heads(wv)
    w_cat = jnp.concatenate([wq, wk, wv], axis=1).astype(BF16)
    nrot = nq + wk.shape[1]
    tables = _rope_tables(seq)
    qkv_l, rot_l = _att_inproj_call(h_lat, modt, w_cat, tables, seq=seq, ctx_row=None, nq=nq, nrot=nrot, tm=tm)
    (qkv_c,) = _att_inproj_call(h_ctx, modt, w_cat, None, seq=n_ctx, ctx_row=ctx_row, nq=nq, nrot=nrot,
                                tm=min(tm, h_ctx.shape[0]))
    o_l = _attn_call(sink, qkv_l, rot_l, qkv_c, bsz=bsz, seq=seq, n_ctx=n_ctx, window=True)
    o_c = _attn_call(sink, qkv_c, None, qkv_c, bsz=bsz, seq=n_ctx, n_ctx=n_ctx, window=False)
    w_o = w_out.astype(BF16)
    h_lat = _outproj_ln_call(o_l, w_o, h_lat, modt, ln_g, ln_b, alpha=alpha, seq=seq, ctx_row=None, tm=tm)
    h_ctx = _outproj_ln_call(o_c, w_o, h_ctx, modt, ln_g, ln_b, alpha=alpha, seq=n_ctx, ctx_row=ctx_row,
                             tm=min(tm, h_ctx.shape[0]))
    return h_lat, h_ctx


def _ssm_layer(h_lat, h_ctx, modt, w_in, conv_w, conv_b, dt_bias, a_log, d_skip, norm_g, w_out, ln_g, ln_b,
               *, bsz, seq, n_ctx, ctx_row, alpha, tm):
    d_inner = w_out.shape[0]
    heads = d_inner // SSM_HEAD_DIM
    gcols = d_inner // SSM_GROUPS
    hpg = heads // SSM_GROUPS
    conv_dim = d_inner + 2 * SSM_GROUPS * SSM_STATE
    n_main = d_inner + conv_dim
    assert hpg == HEADS_PER_GROUP and gcols == SSM_GROUPS * SSM_STATE
    w_main = w_in[:, :n_main].astype(BF16)

    def regroup(v):
        lead = v.shape[:-2]
        v = v.reshape(lead + (2, SSM_GROUPS, hpg))
        v = jnp.moveaxis(v, -3, -2).reshape(lead + (SSM_GROUPS, 2 * hpg))
        return v

    w_dt = regroup(w_in[:, n_main:].reshape(-1, 2, heads)).reshape(-1, 2 * heads)
    w_dt = jnp.pad(w_dt, ((0, 0), (0, LANES - 2 * heads))).astype(BF16)
    dtb = jnp.pad(regroup(dt_bias).reshape(1, 2 * heads), ((0, 0), (0, LANES - 2 * heads)))
    a_log_row = jnp.pad(regroup(a_log).reshape(1, 2 * heads), ((0, 0), (0, LANES - 2 * heads)))
    d_row = jnp.repeat(d_skip, SSM_HEAD_DIM).reshape(1, d_inner)
    ng = norm_g.reshape(1, d_inner)
    ntile_conv = conv_dim // gcols
    cw = jnp.pad(conv_w, ((0, 8 - SSM_CONV_W), (0, 0))).reshape(8, ntile_conv, gcols).transpose(1, 0, 2)
    cb = conv_b.reshape(ntile_conv, 1, gcols)

    n_gate = d_inner // gcols
    gate_l, xbc_l, dt_l = _ssm_inproj_call(h_lat, modt, w_main, w_dt, dtb, cw, cb, n_dt=2 * heads,
                                           n_gate_tiles=n_gate, rows=seq, seq=seq, ctx_row=None, tile=gcols)
    ctx_rows = n_ctx * max(1, min(bsz, tm // n_ctx))
    while (bsz * n_ctx) % ctx_rows:
        ctx_rows -= n_ctx
    _, xbc_c, dt_c = _ssm_inproj_call(h_ctx, modt, w_main, w_dt, dtb, cw, cb, n_dt=2 * heads,
                                      n_gate_tiles=n_gate, rows=ctx_rows, seq=n_ctx, ctx_row=ctx_row, tile=gcols)
    y = _ssd_call(a_log_row, d_row, ng, xbc_c, dt_c, gate_l, xbc_l, dt_l, bsz=bsz, n_ctx=n_ctx, seq=seq)
    return _outproj_ln_call(y, w_out.astype(BF16), h_lat, modt, ln_g, ln_b, alpha=alpha, seq=seq, ctx_row=None, tm=tm)


def kernel(x, c, ctx, c_ctx, w_mod, b_mod, ln_mix_g, ln_mix_b, ln_ff_g, ln_ff_b, att_w_in, att_w_out, att_sink,
           ssm_w_in, ssm_conv_w, ssm_conv_b, ssm_dt_bias, ssm_a_log, ssm_d, ssm_norm_g, ssm_w_out, ff_w1, ff_w2):
    bsz, seq, d = x.shape
    n_ctx = ctx.shape[1]
    depth = w_mod.shape[0]
    alpha = (2.0 * depth) ** 0.25
    tm = 1024 if seq % 1024 == 0 and (bsz * n_ctx) % 1024 == 0 else 512
    assert seq % tm == 0 and (bsz * n_ctx) % min(tm, bsz * n_ctx) == 0 and seq % PERM_BLOCK == 0
    ctx_row = bsz
    rows = -(-(bsz + 1) // MOD_ROWS_ALIGN) * MOD_ROWS_ALIGN
    cc = jnp.concatenate([c, c_ctx[None, :], jnp.zeros((rows - bsz - 1, d), F32)], axis=0)
    mod = _mod_call(cc, w_mod, b_mod)
    h_lat = x.reshape(bsz * seq, d)
    h_ctx = ctx.reshape(bsz * n_ctx, d)
    for i in range(depth):
        last = i == depth - 1
        j = i // N_MIXERS
        modt = mod[i].reshape(rows * N_MOD, 1, d)
        if i % N_MIXERS == 0:
            h_lat, h_ctx_mix = _attention_layer(h_lat, h_ctx, modt, att_w_in[j], att_w_out[j], att_sink[j],
                                                ln_mix_g[i], ln_mix_b[i], bsz=bsz, seq=seq, n_ctx=n_ctx,
                                                ctx_row=ctx_row, alpha=alpha, tm=tm)
        else:
            if not last:
                raise NotImplementedError("context outputs of an SSD layer are only needed when it is not last")
            h_lat = _ssm_layer(h_lat, h_ctx, modt, ssm_w_in[j], ssm_conv_w[j], ssm_conv_b[j], ssm_dt_bias[j],
                               ssm_a_log[j], ssm_d[j], ssm_norm_g[j], ssm_w_out[j], ln_mix_g[i], ln_mix_b[i],
                               bsz=bsz, seq=seq, n_ctx=n_ctx, ctx_row=ctx_row, alpha=alpha, tm=tm)
            h_ctx_mix = None
        w1 = ff_w1[i].astype(BF16)
        w2 = ff_w2[i].astype(BF16)
        h_lat = _mlp_ln_call(h_lat, modt, w1, w2, ln_ff_g[i], ln_ff_b[i], alpha=alpha, seq=seq, ctx_row=None,
                             tm=tm, tf=1024)
        if not last:
            h_ctx = _mlp_ln_call(h_ctx_mix, modt, w1, w2, ln_ff_g[i], ln_ff_b[i], alpha=alpha, seq=n_ctx,
                                 ctx_row=ctx_row, tm=min(tm, h_ctx_mix.shape[0]), tf=1024)
    return h_lat.reshape(bsz, seq, d)
```

```python
import functools

import jax
import jax.numpy as jnp
from jax import lax
from jax.experimental import pallas as pl
from jax.experimental.pallas import tpu as pltpu

F32 = jnp.float32
BF16 = jnp.bfloat16

GRID_W = 64
N_MIXERS = 2
ATT_HEADS = 16
ATT_KV_HEADS = 4
HEAD_DIM = 64
ATT_GROUP = ATT_HEADS // ATT_KV_HEADS
WINDOW = 128
BLOCK = 128
ROPE_BASE = 10000.0
ROPE_AXIS_DIM = HEAD_DIM // 2

SSM_HEAD_DIM = 64
SSM_GROUPS = 4
SSM_STATE = 128
SSM_CONV_W = 5
SSM_CHUNK = 128

LOG2E = 1.4426950408889634
Q_SCALE = HEAD_DIM ** -0.5 * LOG2E

N_MOD = 6
LN_EPS = 1e-5
RMS_EPS = 1e-5
NEG_INF = -1e30

LANES = 128
MOD_ROWS_ALIGN = 16
ROW_SPLIT = 4
VMEM_LIMIT = 48 * 1024 * 1024


def _cparams(sem):
    return pltpu.CompilerParams(dimension_semantics=sem, vmem_limit_bytes=VMEM_LIMIT)


def _sigmoid(x):
    return 1.0 / (1.0 + jnp.exp(-x))


def _dot(a, b):
    return jnp.dot(a, b, preferred_element_type=F32)


def _dot_nt(a, b):
    return lax.dot_general(a, b, (((1,), (1,)), ((), ())), preferred_element_type=F32)


def _dot_tn(a, b):
    return lax.dot_general(a, b, (((0,), (0,)), ((), ())), preferred_element_type=F32)


def _layer_norm_rows(v, g, b):
    mu = jnp.mean(v, axis=-1, keepdims=True)
    d = v - mu
    var = jnp.mean(d * d, axis=-1, keepdims=True)
    return d * lax.rsqrt(var + LN_EPS) * g + b


def _mod_kernel(c_ref, w_ref, b_ref, o_ref):
    c = c_ref[...]
    s = (c * _sigmoid(c)).astype(BF16)
    o_ref[0] = _dot(s, w_ref[0].astype(BF16)) + b_ref[0]


def _mod_call(cc, w_mod, b_mod):
    depth, d, n = w_mod.shape
    rows = cc.shape[0]
    tn = 1024
    return pl.pallas_call(
        _mod_kernel,
        grid=(depth, n // tn),
        in_specs=[
            pl.BlockSpec((rows, d), lambda l, j: (0, 0)),
            pl.BlockSpec((1, d, tn), lambda l, j: (l, 0, j)),
            pl.BlockSpec((1, 1, tn), lambda l, j: (l, 0, j)),
        ],
        out_specs=pl.BlockSpec((1, rows, tn), lambda l, j: (l, 0, j)),
        out_shape=jax.ShapeDtypeStruct((depth, rows, n), F32),
        compiler_params=_cparams(("arbitrary", "arbitrary")),
        name="adaln_mod",
    )(cc, w_mod, b_mod.reshape(depth, 1, n))


def _mod_index(which, tm, seq, ctx_row):
    if ctx_row is None:
        return lambda i, *_: ((i * tm // seq) * N_MOD + which, 0, 0)
    return lambda i, *_: (ctx_row * N_MOD + which, 0, 0)


def _att_inproj_kernel(*refs, rope, nq):
    if rope:
        x_ref, sh_ref, sc_ref, w_ref, cos_ref, sin_ref, qkv_ref, rot_ref = refs
    else:
        x_ref, sh_ref, sc_ref, w_ref, qkv_ref = refs
    u = (x_ref[...] * (1.0 + sc_ref[0]) + sh_ref[0]).astype(BF16)
    acc = _dot(u, w_ref[...])
    n = acc.shape[1]
    nrot = rot_ref.shape[1] if rope else 0
    if rope:
        cos = cos_ref[...]
        sin = sin_ref[...]
        lane = lax.broadcasted_iota(jnp.int32, cos.shape, 1)
        first_half = (lane & (ROPE_AXIS_DIM - 1)) < (ROPE_AXIS_DIM // 2)
    for cidx in range(n // LANES):
        chunk = acc[:, cidx * LANES:(cidx + 1) * LANES]
        if cidx * LANES < nq:
            chunk = chunk * Q_SCALE
        qkv_ref[:, cidx * LANES:(cidx + 1) * LANES] = chunk.astype(BF16)
        if rope and cidx * LANES < nrot:
            half = ROPE_AXIS_DIM // 2
            swapped = jnp.where(first_half, pltpu.roll(chunk, LANES - half, 1), pltpu.roll(chunk, half, 1))
            rot_ref[:, cidx * LANES:(cidx + 1) * LANES] = (chunk * cos + swapped * sin).astype(BF16)


def _att_inproj_call(h, modt, w, tables, *, seq, ctx_row, nq, nrot, tm):
    t, d = h.shape
    n = w.shape[1]
    rope = tables is not None
    in_specs = [
        pl.BlockSpec((tm, d), lambda i: (i, 0)),
        pl.BlockSpec((1, 1, d), _mod_index(0, tm, seq, ctx_row)),
        pl.BlockSpec((1, 1, d), _mod_index(1, tm, seq, ctx_row)),
        pl.BlockSpec((d, n), lambda i: (0, 0)),
    ]
    args = [h, modt, modt, w]
    out_specs = [pl.BlockSpec((tm, n), lambda i: (i, 0))]
    out_shape = [jax.ShapeDtypeStruct((t, n), BF16)]
    if rope:
        per_seq = seq // tm
        in_specs += [pl.BlockSpec((tm, LANES), lambda i: (i % per_seq, 0))] * 2
        args += list(tables)
        out_specs.append(pl.BlockSpec((tm, nrot), lambda i: (i, 0)))
        out_shape.append(jax.ShapeDtypeStruct((t, nrot), BF16))
    return pl.pallas_call(
        functools.partial(_att_inproj_kernel, rope=rope, nq=nq),
        grid=(t // tm,),
        in_specs=in_specs,
        out_specs=out_specs,
        out_shape=out_shape,
        compiler_params=_cparams(("arbitrary",)),
        name="att_inproj_rope" if rope else "att_inproj",
    )(*args)


def _attn_kernel(*refs, window, nb):
    if window:
        (sink_ref, qr_ref, qp_ref, kp_ref, kc_ref, kn_ref, vp_ref, vc_ref, vn_ref,
         kx_ref, vx_ref, o_ref) = refs
    else:
        sink_ref, qp_ref, kx_ref, vx_ref, o_ref = refs
    i = pl.program_id(1)
    rows = 2 * BLOCK
    half_lane = lax.broadcasted_iota(jnp.int32, (BLOCK, LANES), 1) < HEAD_DIM
    ridx = lax.broadcasted_iota(jnp.int32, (rows, 1), 0)
    if window:
        li = lax.broadcasted_iota(jnp.int32, (BLOCK, BLOCK), 0)
        ci = lax.broadcasted_iota(jnp.int32, (BLOCK, BLOCK), 1)
        ok_prev = ci >= li + jnp.where(i > 0, 0, BLOCK)
        ok_next = ci + jnp.where(i < nb - 1, 0, BLOCK) <= li

    def masked(s_blk, ok):
        return jnp.concatenate([jnp.where(ok, s_blk[r * BLOCK:(r + 1) * BLOCK], NEG_INF)
                                for r in range(rows // BLOCK)], axis=0)

    def with_ones(v):
        return jnp.concatenate([v, jnp.ones_like(v)], axis=1)

    pairs_per_group = ATT_GROUP // 2

    def stack_heads(q_ref, item):
        pair = q_ref[0, :, item * LANES:(item + 1) * LANES]
        zero = jnp.zeros_like(pair)
        return jnp.concatenate([jnp.where(half_lane, pair, zero), jnp.where(half_lane, zero, pair)],
                               axis=0)

    def scores(item):
        g = item // pairs_per_group
        gl = slice(g * LANES, (g + 1) * LANES)
        s_ctx = _dot_nt(stack_heads(qp_ref, item), kx_ref[0, :, gl])
        if not window:
            return s_ctx, None
        kw = jnp.concatenate([kp_ref[0, :, gl], kc_ref[0, :, gl], kn_ref[0, :, gl]], axis=0)
        s_win = _dot_nt(stack_heads(qr_ref, item), kw)
        s_win = jnp.concatenate([masked(s_win[:, :BLOCK], ok_prev), s_win[:, BLOCK:2 * BLOCK],
                                 masked(s_win[:, 2 * BLOCK:], ok_next)], axis=1)
        return s_ctx, s_win

    n_items = ATT_HEADS // 2
    ahead = scores(0)
    for item in range(n_items):
        s_ctx, s_win = ahead
        if item + 1 < n_items:
            ahead = scores(item + 1)
        g = item // pairs_per_group
        gl = slice(g * LANES, (g + 1) * LANES)
        sink_col = jnp.where(ridx < BLOCK, sink_ref[2 * item] * LOG2E, sink_ref[2 * item + 1] * LOG2E)
        pieces = [s_ctx[:, k * LANES:(k + 1) * LANES] for k in range(s_ctx.shape[1] // LANES)]
        if window:
            pieces += [s_win[:, k * BLOCK:(k + 1) * BLOCK] for k in range(3)]
        mx = pieces[0]
        for piece in pieces[1:]:
            mx = jnp.maximum(mx, piece)
        m = jnp.maximum(jnp.max(mx, axis=-1, keepdims=True), sink_col)
        p_ctx = jnp.exp2(s_ctx - m)
        if window:
            vw = jnp.concatenate([vp_ref[0, :, gl], vc_ref[0, :, gl], vn_ref[0, :, gl]], axis=0)
            acc = (_dot(p_ctx.astype(BF16), with_ones(vx_ref[0, :, gl]))
                   + _dot(jnp.exp2(s_win - m).astype(BF16), with_ones(vw)))
            denom = acc[:, LANES:LANES + 1] + jnp.exp2(sink_col - m)
            o = acc[:, :LANES] * (1.0 / denom)
        else:
            denom = jnp.sum(p_ctx, axis=-1, keepdims=True) + jnp.exp2(sink_col - m)
            o = _dot(p_ctx.astype(BF16), vx_ref[0, :, gl]) * (1.0 / denom)
        o_ref[0, :, item * LANES:(item + 1) * LANES] = jnp.where(half_lane, o[:BLOCK], o[BLOCK:]).astype(BF16)


def _attn_call(sink, qkv, rot, qkv_ctx, *, bsz, seq, n_ctx, window):
    nq = ATT_HEADS * HEAD_DIM
    nkv = ATT_KV_HEADS * LANES
    nb = seq // BLOCK
    qkv3 = qkv.reshape(bsz, seq, qkv.shape[1])
    ctx3 = qkv_ctx.reshape(bsz, n_ctx, qkv_ctx.shape[1])
    kcol = nq // nkv
    vcol = kcol + 1
    smem = pl.BlockSpec(memory_space=pltpu.SMEM)
    qspec = pl.BlockSpec((1, BLOCK, nq), lambda b, i: (b, i, 0))
    kx = pl.BlockSpec((1, n_ctx, nkv), lambda b, i: (b, 0, kcol))
    vx = pl.BlockSpec((1, n_ctx, nkv), lambda b, i: (b, 0, vcol))
    if window:
        rot3 = rot.reshape(bsz, seq, rot.shape[1])

        def kv_specs(colblk):
            return [
                pl.BlockSpec((1, BLOCK, nkv), lambda b, i: (b, jnp.maximum(i - 1, 0), colblk)),
                pl.BlockSpec((1, BLOCK, nkv), lambda b, i: (b, i, colblk)),
                pl.BlockSpec((1, BLOCK, nkv), lambda b, i: (b, jnp.minimum(i + 1, nb - 1), colblk)),
            ]
        in_specs = [smem, qspec, qspec] + kv_specs(kcol) + kv_specs(vcol) + [kx, vx]
        args = [sink, rot3, qkv3, rot3, rot3, rot3, qkv3, qkv3, qkv3, ctx3, ctx3]
    else:
        in_specs = [smem, qspec, kx, vx]
        args = [sink, qkv3, ctx3, ctx3]
    out = pl.pallas_call(
        functools.partial(_attn_kernel, window=window, nb=nb),
        grid=(bsz, nb),
        in_specs=in_specs,
        out_specs=pl.BlockSpec((1, BLOCK, nq), lambda b, i: (b, i, 0)),
        out_shape=jax.ShapeDtypeStruct((bsz, seq, nq), BF16),
        compiler_params=_cparams(("arbitrary", "arbitrary")),
        name="attn_window" if window else "attn_ctx",
    )(*args)
    return out.reshape(bsz * seq, nq)


def _outproj_ln_kernel(a_ref, w_ref, h_ref, gate_ref, g_ref, b_ref, o_ref, *, alpha):
    sub = a_ref.shape[0] // ROW_SPLIT
    for k in range(ROW_SPLIT):
        rows = slice(k * sub, (k + 1) * sub)
        y = _dot(a_ref[rows, :], w_ref[...])
        v = alpha * h_ref[rows, :] + gate_ref[0] * y
        o_ref[rows, :] = _layer_norm_rows(v, g_ref[...], b_ref[...])


def _outproj_ln_call(a, w, h, modt, ln_g, ln_b, *, alpha, seq, ctx_row, tm):
    t, k = a.shape
    d = w.shape[1]
    return pl.pallas_call(
        functools.partial(_outproj_ln_kernel, alpha=alpha),
        grid=(t // tm,),
        in_specs=[
            pl.BlockSpec((tm, k), lambda i: (i, 0)),
            pl.BlockSpec((k, d), lambda i: (0, 0)),
            pl.BlockSpec((tm, d), lambda i: (i, 0)),
            pl.BlockSpec((1, 1, d), _mod_index(2, tm, seq, ctx_row)),
            pl.BlockSpec((1, d), lambda i: (0, 0)),
            pl.BlockSpec((1, d), lambda i: (0, 0)),
        ],
        out_specs=pl.BlockSpec((tm, d), lambda i: (i, 0)),
        out_shape=jax.ShapeDtypeStruct((t, d), F32),
        compiler_params=_cparams(("arbitrary",)),
        name="outproj_ln",
    )(a, w, h, modt, ln_g.reshape(1, d), ln_b.reshape(1, d))


def _mlp_ln_kernel(h_ref, sh_ref, sc_ref, gate_ref, w1_ref, w2_ref, g_ref, b_ref, o_ref, u_scr, acc_scr, *, alpha):
    j = pl.program_id(1)

    @pl.when(j == 0)
    def _():
        u_scr[...] = (h_ref[...] * (1.0 + sc_ref[0]) + sh_ref[0]).astype(BF16)
        acc_scr[...] = jnp.zeros_like(acc_scr)

    last = pl.num_programs(1) - 1

    def hidden(rows):
        a = jnp.maximum(_dot(u_scr[rows, :], w1_ref[...]), 0.0)
        return _dot((a * a).astype(BF16), w2_ref[...])

    @pl.when(j < last)
    def _():
        acc_scr[...] += hidden(slice(None))

    @pl.when(j == last)
    def _():
        sub = h_ref.shape[0] // ROW_SPLIT
        for k in range(ROW_SPLIT):
            rows = slice(k * sub, (k + 1) * sub)
            v = alpha * h_ref[rows, :] + gate_ref[0] * (acc_scr[rows, :] + hidden(rows))
            o_ref[rows, :] = _layer_norm_rows(v, g_ref[...], b_ref[...])


def _mlp_ln_call(h, modt, w1, w2, ln_g, ln_b, *, alpha, seq, ctx_row, tm, tf):
    t, d = h.shape
    ff = w1.shape[1]
    return pl.pallas_call(
        functools.partial(_mlp_ln_kernel, alpha=alpha),
        grid=(t // tm, ff // tf),
        in_specs=[
            pl.BlockSpec((tm, d), lambda i, j: (i, 0)),
            pl.BlockSpec((1, 1, d), _mod_index(3, tm, seq, ctx_row)),
            pl.BlockSpec((1, 1, d), _mod_index(4, tm, seq, ctx_row)),
            pl.BlockSpec((1, 1, d), _mod_index(5, tm, seq, ctx_row)),
            pl.BlockSpec((d, tf), lambda i, j: (0, j)),
            pl.BlockSpec((tf, d), lambda i, j: (j, 0)),
            pl.BlockSpec((1, d), lambda i, j: (0, 0)),
            pl.BlockSpec((1, d), lambda i, j: (0, 0)),
        ],
        out_specs=pl.BlockSpec((tm, d), lambda i, j: (i, 0)),
        out_shape=jax.ShapeDtypeStruct((t, d), F32),
        scratch_shapes=[pltpu.VMEM((tm, d), BF16), pltpu.VMEM((tm, d), F32)],
        compiler_params=_cparams(("arbitrary", "arbitrary")),
        name="mlp_ln",
    )(h, modt, modt, modt, w1, w2, ln_g.reshape(1, d), ln_b.reshape(1, d))


SUBLANES = 8
PERM_BLOCK = SSM_CHUNK
PERM_VREGS = PERM_BLOCK // SUBLANES
PROJ_ROWS = PERM_BLOCK


def _token_of_row(r):
    return (r & (SUBLANES - 1)) * PERM_VREGS + (r >> 3)


def _perm_matrix(inverse):
    rr = lax.broadcasted_iota(jnp.int32, (PERM_BLOCK, PERM_BLOCK), 0)
    cc = lax.broadcasted_iota(jnp.int32, (PERM_BLOCK, PERM_BLOCK), 1)
    hit = (rr == _token_of_row(cc)) if inverse else (cc == _token_of_row(rr))
    return jnp.where(hit, 1.0, 0.0).astype(BF16)


def _ssm_inproj_kernel(h_ref, sh_ref, sc_ref, wg_ref, w_ref, wdt_ref, dtb_ref, cw_ref, cb_ref,
                       gate_ref, o_ref, dt_ref, u_scr, *, n_dt, n_gate_tiles, seq):
    j = pl.program_id(1)
    rows_per_step = h_ref.shape[0]
    tile = o_ref.shape[2]
    nblk = rows_per_step // PERM_BLOCK
    seq_blocks = seq // PERM_BLOCK
    nv = PERM_VREGS

    @pl.when(j == 0)
    def _():
        perm = _perm_matrix(False)
        for k in range(nblk):
            rows = slice(k * PERM_BLOCK, (k + 1) * PERM_BLOCK)
            u = (h_ref[rows, :] * (1.0 + sc_ref[0]) + sh_ref[0]).astype(BF16)
            u_scr[rows, :] = _dot(perm, u).astype(BF16)
        raw = _dot(u_scr[...], wdt_ref[...]) + dtb_ref[...]
        sp = jnp.maximum(raw, 0.0) + jnp.log1p(jnp.exp(-jnp.abs(raw)))
        lane = lax.broadcasted_iota(jnp.int32, sp.shape, 1)
        dt_ref[...] = jnp.where(lane < n_dt, sp, 0.0)

    def conv_tile(with_gate):
        w = cw_ref[0]
        bias = cb_ref[0]
        row = lax.broadcasted_iota(jnp.int32, (SUBLANES, tile), 0)
        zeros = jnp.zeros((SUBLANES, tile), F32)
        blocks = [None] * nblk

        def up(v, nxt):
            return jnp.where(row == SUBLANES - 1, pltpu.roll(nxt, SUBLANES - 1, 0), pltpu.roll(v, SUBLANES - 1, 0))

        def down(v, prv):
            return jnp.where(row == 0, pltpu.roll(prv, 1, 0), pltpu.roll(v, 1, 0))

        def conv(k):
            x = blocks[k]
            nxt = blocks[k + 1] if (k + 1) % seq_blocks else None
            prv = blocks[k - 1] if k % seq_blocks else None
            u0 = up(x[0], zeros if nxt is None else nxt[0])
            u1 = up(x[1], zeros if nxt is None else nxt[1])
            d0 = down(x[nv - 2], zeros if prv is None else prv[nv - 2])
            d1 = down(x[nv - 1], zeros if prv is None else prv[nv - 1])
            shifted = (
                jnp.concatenate([d0[None], d1[None], x[:nv - 2]], axis=0),
                jnp.concatenate([d1[None], x[:nv - 1]], axis=0),
                x,
                jnp.concatenate([x[1:], u0[None]], axis=0),
                jnp.concatenate([x[2:], u0[None], u1[None]], axis=0),
            )
            acc = jnp.zeros((nv, SUBLANES, tile), F32) + bias
            for t in range(SSM_CONV_W):
                acc = acc + w[t:t + 1] * shifted[t]
            res = (acc * _sigmoid(acc)).astype(BF16)
            o_ref[0, k * PERM_BLOCK:(k + 1) * PERM_BLOCK, :] = res.reshape(PERM_BLOCK, tile)

        prows = min(PROJ_ROWS, rows_per_step)
        per_proj = prows // PERM_BLOCK

        def project(p):
            rows = slice(p * prows, (p + 1) * prows)
            r = _dot(u_scr[rows, :], w_ref[...])
            for i in range(per_proj):
                blocks[p * per_proj + i] = r[i * PERM_BLOCK:(i + 1) * PERM_BLOCK].reshape(nv, SUBLANES, tile)
            if with_gate:
                r = _dot(u_scr[rows, :], wg_ref[...])
                gate_ref[0, rows, :] = (r * _sigmoid(r)).astype(BF16)

        project(0)
        for p in range(rows_per_step // prows):
            if (p + 1) * prows < rows_per_step:
                project(p + 1)
            for i in range(per_proj):
                conv(p * per_proj + i)

    @pl.when(j < n_gate_tiles)
    def _():
        conv_tile(True)

    @pl.when(j >= n_gate_tiles)
    def _():
        conv_tile(False)


def _ssm_inproj_call(h, modt, w_main, w_dt, dt_bias, conv_w, conv_b, *, n_dt, n_gate_tiles, rows, seq, ctx_row, tile):
    t, d = h.shape
    assert rows % seq == 0 and t % rows == 0 and (ctx_row is not None or rows == seq)
    n_conv_tiles = w_main.shape[1] // tile - n_gate_tiles
    assert n_conv_tiles >= n_gate_tiles
    last_gate = n_gate_tiles - 1
    return pl.pallas_call(
        functools.partial(_ssm_inproj_kernel, n_dt=n_dt, n_gate_tiles=n_gate_tiles, seq=seq),
        grid=(t // rows, n_conv_tiles),
        in_specs=[
            pl.BlockSpec((rows, d), lambda b, j: (b, 0)),
            pl.BlockSpec((1, 1, d), _mod_index(0, rows, seq, ctx_row)),
            pl.BlockSpec((1, 1, d), _mod_index(1, rows, seq, ctx_row)),
            pl.BlockSpec((d, tile), lambda b, j: (0, jnp.minimum(j, last_gate))),
            pl.BlockSpec((d, tile), lambda b, j: (0, n_gate_tiles + j)),
            pl.BlockSpec((d, LANES), lambda b, j: (0, 0)),
            pl.BlockSpec((1, LANES), lambda b, j: (0, 0)),
            pl.BlockSpec((1, 8, tile), lambda b, j: (j, 0, 0)),
            pl.BlockSpec((1, 1, tile), lambda b, j: (j, 0, 0)),
        ],
        out_specs=[
            pl.BlockSpec((1, rows, tile), lambda b, j: (jnp.minimum(j, last_gate), b, 0)),
            pl.BlockSpec((1, rows, tile), lambda b, j: (j, b, 0)),
            pl.BlockSpec((rows, LANES), lambda b, j: (b, 0)),
        ],
        out_shape=[
            jax.ShapeDtypeStruct((n_gate_tiles, t, tile), BF16),
            jax.ShapeDtypeStruct((n_conv_tiles, t, tile), BF16),
            jax.ShapeDtypeStruct((t, LANES), F32),
        ],
        scratch_shapes=[pltpu.VMEM((rows, d), BF16)],
        compiler_params=_cparams(("arbitrary", "arbitrary")),
        name="ssm_inproj_conv",
    )(h, modt, modt, w_main, w_main, w_dt, dt_bias, conv_w, conv_b)


HEADS_PER_GROUP = 8
DT_LANES_PER_GROUP = 2 * HEADS_PER_GROUP


def _expand_heads(v, lane0, nrows):
    rows = max(nrows, 8)
    if rows != nrows:
        v = jnp.broadcast_to(v, (rows, LANES))
    lane = lax.broadcasted_iota(jnp.int32, (rows, LANES), 1)
    parts = []
    for p in range(HEADS_PER_GROUP // 2):
        idx = lane0 + 2 * p + (lane >> 6)
        parts.append(jnp.take_along_axis(v, idx, axis=1, mode="promise_in_bounds"))
    return jnp.concatenate(parts, axis=1)[:nrows]


def _cumsum_rows(a):
    q = a.shape[0]
    tril = (_token_of_row(lax.broadcasted_iota(jnp.int32, (q, q), 0))
            >= _token_of_row(lax.broadcasted_iota(jnp.int32, (q, q), 1)))
    tril = jnp.where(tril, 1.0, 0.0).astype(BF16)
    hi = a.astype(BF16)
    r1 = a - hi.astype(F32)
    mid = r1.astype(BF16)
    lo = (r1 - mid.astype(F32)).astype(BF16)
    return _dot(tril, hi) + _dot(tril, mid) + _dot(tril, lo)


def _chunk_terms(dt, a_neg):
    q = dt.shape[0]
    lane = lax.broadcasted_iota(jnp.int32, (q, LANES), 1)
    is_bwd = (lane & HEADS_PER_GROUP) != 0
    a = dt * a_neg
    acs = _cumsum_rows(a)
    tot = acs[q - 1:q, :]
    e = jnp.where(is_bwd, acs - a, acs)
    w_state = dt * jnp.exp(jnp.where(is_bwd, e, tot - acs))
    f_out = jnp.exp(jnp.where(is_bwd, tot - e, acs))
    dec = jnp.exp(tot)
    return e, w_state, f_out, dec


def _state_update(s_scr, idx, x, bm, w_state, dec, lane0):
    q = x.shape[0]
    xw = (x.astype(F32) * _expand_heads(w_state, lane0, q)).astype(BF16)
    s_scr[idx] = s_scr[idx] * _expand_heads(dec, lane0, 1) + _dot_tn(bm, xw)


def _ssd_kernel(alog_ref, d_ref, ng_ref, xcb_ref, dtcb_ref, xcf_ref, dtcf_ref, xb_ref, dtb_ref,
                xf_ref, zf_ref, dtf_ref, y_ref, sf_scr, sb_scr, sbs_scr, *, nb, ncc, ncl):
    o = pl.program_id(0)
    s = pl.program_id(1)
    par = o % 2
    prev = 1 - par
    entering = o < nb
    emitting = o >= 1
    lat = s >= ncc
    t = s - ncc
    a_neg = -jnp.exp(alog_ref[...])
    q = SSM_CHUNK
    n = SSM_STATE
    gcols = HEADS_PER_GROUP * SSM_HEAD_DIM

    def sweep_steps(s_scr, slot, x_ref, dt_ref, dir_off, keep=None):
        _, w_state, _, dec = _chunk_terms(dt_ref[...], a_neg)

        def one(g):
            idx = g if slot is None else (slot, g)
            if keep is not None:
                sbs_scr[par, keep, g] = s_scr[idx].astype(BF16)
            _state_update(s_scr, idx, x_ref[g], x_ref[SSM_GROUPS, :, g * n:(g + 1) * n], w_state, dec,
                          g * DT_LANES_PER_GROUP + dir_off)
        return [functools.partial(one, g) for g in range(SSM_GROUPS)]

    @pl.when((s == 0) & entering)
    def _():
        sf_scr[par] = jnp.zeros(sf_scr.shape[1:], F32)
        sb_scr[...] = jnp.zeros_like(sb_scr)

    @pl.when((s < ncc) & entering)
    def _():
        back = sweep_steps(sb_scr, None, xcb_ref, dtcb_ref, HEADS_PER_GROUP)
        fore = sweep_steps(sf_scr, par, xcf_ref, dtcf_ref, 0)
        for g in range(SSM_GROUPS):
            back[g]()
            fore[g]()

    def backward_steps():
        return sweep_steps(sb_scr, None, xb_ref, dtb_ref, HEADS_PER_GROUP, keep=ncl - 1 - t)

    def forward(between=None):
        dt = dtf_ref[...]
        e, w_state, f_out, dec = _chunk_terms(dt, a_neg)
        e_t = e.T
        dt_t = dt.T
        li = _token_of_row(lax.broadcasted_iota(jnp.int32, (q, q), 0))
        si = _token_of_row(lax.broadcasted_iota(jnp.int32, (q, q), 1))
        unperm = _perm_matrix(True)
        lower = li >= si
        below = li > si
        above = li < si
        sign = jnp.where(lower, 1.0, -1.0)
        dir_lane = jnp.where(lower, 0, HEADS_PER_GROUP)
        first = lax.broadcasted_iota(jnp.int32, (q, LANES), 1) < SSM_HEAD_DIM
        for g in range(SSM_GROUPS):
            lf = g * DT_LANES_PER_GROUP
            lb = lf + HEADS_PER_GROUP
            x = xf_ref[g]
            bm = xf_ref[SSM_GROUPS, :, g * n:(g + 1) * n]
            cm = xf_ref[SSM_GROUPS + 1, :, g * n:(g + 1) * n]
            cb = _dot_nt(cm, bm)
            y_f = _dot(cm, sf_scr[prev, g].astype(BF16))
            y_b = _dot(cm, sbs_scr[prev, t, g])
            y_parts = []
            for p in range(HEADS_PER_GROUP // 2):
                ms = []
                for r in (2 * p, 2 * p + 1):
                    e_col = jnp.take_along_axis(e, dir_lane + (lf + r), axis=1, mode="promise_in_bounds")
                    e_row = jnp.where(lower, e_t[lf + r:lf + r + 1, :], e_t[lb + r:lb + r + 1, :])
                    dt_f = dt_t[lf + r:lf + r + 1, :]
                    dt_b = dt_t[lb + r:lb + r + 1, :]
                    dsel = jnp.where(below, dt_f, jnp.where(above, dt_b, dt_f + dt_b))
                    ms.append((cb * jnp.exp((e_col - e_row) * sign) * dsel).astype(BF16))
                xp = x[:, p * LANES:(p + 1) * LANES]
                zero = jnp.zeros_like(xp)
                x_bd = jnp.concatenate([jnp.where(first, xp, zero), jnp.where(first, zero, xp)], axis=0)
                y_parts.append(_dot(jnp.concatenate(ms, axis=1), x_bd))
            cols = slice(g * gcols, (g + 1) * gcols)
            y = (jnp.concatenate(y_parts, axis=1)
                 + y_f * _expand_heads(f_out, lf, q) + y_b * _expand_heads(f_out, lb, q)
                 + x.astype(F32) * d_ref[:, cols])
            yy = y * zf_ref[g].astype(F32)
            ms_ = jnp.mean(yy * yy, axis=-1, keepdims=True)
            y_out = (yy * lax.rsqrt(ms_ + RMS_EPS) * ng_ref[:, cols]).astype(BF16)
            y_ref[:, cols] = _dot(unperm, y_out).astype(BF16)
            _state_update(sf_scr, (prev, g), x, bm, w_state, dec, lf)
            if between is not None:
                between[g]()

    @pl.when(lat & entering & emitting)
    def _():
        forward(between=backward_steps())

    @pl.when(lat & jnp.logical_not(emitting))
    def _():
        for step in backward_steps():
            step()

    @pl.when(lat & jnp.logical_not(entering))
    def _():
        forward()


def _ssd_call(a_log, d_row, norm_g, xbc_c, dt_c, gate_l, xbc_l, dt_l, *, bsz, n_ctx, seq):
    q = SSM_CHUNK
    ncc = n_ctx // q
    ncl = seq // q
    gcols = xbc_l.shape[2]
    d_inner = SSM_GROUPS * gcols
    nsteps = ncc + ncl
    n_state = SSM_STATE

    def entering(o):
        return jnp.minimum(o, bsz - 1)

    def emitting(o):
        return jnp.maximum(o - 1, 0)

    def bchunk(s):
        return ncl - 1 - jnp.maximum(s - ncc, 0)

    def fchunk(s):
        return jnp.maximum(s - ncc, 0)

    def cbrow(o, s):
        return entering(o) * ncc + jnp.maximum(ncc - 1 - s, 0)

    def cfrow(o, s):
        return entering(o) * ncc + jnp.minimum(s, ncc - 1)

    def brow(o, s):
        return entering(o) * ncl + bchunk(s)

    def frow(o, s):
        return emitting(o) * ncl + fchunk(s)

    def orow(o, s):
        return jnp.where(o == 0, 0, frow(o, s))

    assert SSM_GROUPS * n_state == gcols
    cspec = (xbc_l.shape[0], q, gcols)
    in_specs = [
        pl.BlockSpec((1, LANES), lambda o, s: (0, 0)),
        pl.BlockSpec((1, d_inner), lambda o, s: (0, 0)),
        pl.BlockSpec((1, d_inner), lambda o, s: (0, 0)),
        pl.BlockSpec(cspec, lambda o, s: (0, cbrow(o, s), 0)),
        pl.BlockSpec((q, LANES), lambda o, s: (cbrow(o, s), 0)),
        pl.BlockSpec(cspec, lambda o, s: (0, cfrow(o, s), 0)),
        pl.BlockSpec((q, LANES), lambda o, s: (cfrow(o, s), 0)),
        pl.BlockSpec(cspec, lambda o, s: (0, brow(o, s), 0)),
        pl.BlockSpec((q, LANES), lambda o, s: (brow(o, s), 0)),
        pl.BlockSpec(cspec, lambda o, s: (0, frow(o, s), 0)),
        pl.BlockSpec((SSM_GROUPS, q, gcols), lambda o, s: (0, frow(o, s), 0)),
        pl.BlockSpec((q, LANES), lambda o, s: (frow(o, s), 0)),
    ]
    return pl.pallas_call(
        functools.partial(_ssd_kernel, nb=bsz, ncc=ncc, ncl=ncl),
        grid=(bsz + 1, nsteps),
        in_specs=in_specs,
        out_specs=pl.BlockSpec((q, d_inner), lambda o, s: (orow(o, s), 0)),
        out_shape=jax.ShapeDtypeStruct((bsz * seq, d_inner), BF16),
        scratch_shapes=[
            pltpu.VMEM((2, SSM_GROUPS, n_state, gcols), F32),
            pltpu.VMEM((SSM_GROUPS, n_state, gcols), F32),
            pltpu.VMEM((2, ncl, SSM_GROUPS, n_state, gcols), BF16),
        ],
        compiler_params=_cparams(("arbitrary", "arbitrary")),
        name="ssd_scan",
    )(a_log, d_row, norm_g, xbc_c, dt_c, xbc_c, dt_c, xbc_l, dt_l, xbc_l, gate_l, dt_l)


def _rope_tables(seq):
    rows = seq // GRID_W
    row = jnp.repeat(jnp.arange(rows), GRID_W).astype(F32)
    col = jnp.tile(jnp.arange(GRID_W), rows).astype(F32)
    inv_freq = ROPE_BASE ** (-jnp.arange(0, ROPE_AXIS_DIM, 2, dtype=F32) / ROPE_AXIS_DIM)
    ang_r = row[:, None] * inv_freq[None, :]
    ang_c = col[:, None] * inv_freq[None, :]
    cos = jnp.concatenate([jnp.cos(ang_r)] * 2 + [jnp.cos(ang_c)] * 2, axis=-1)
    sin = jnp.concatenate([-jnp.sin(ang_r), jnp.sin(ang_r), -jnp.sin(ang_c), jnp.sin(ang_c)], axis=-1)
    reps = LANES // HEAD_DIM
    return jnp.tile(cos, (1, reps)), jnp.tile(sin, (1, reps))


def _dup_heads(w):
    d, n = w.shape
    w = w.reshape(d, n // HEAD_DIM, 1, HEAD_DIM)
    return jnp.broadcast_to(w, (d, n // HEAD_DIM, LANES // HEAD_DIM, HEAD_DIM)).reshape(d, -1)


def _attention_layer(h_lat, h_ctx, modt, w_in, w_out, sink, ln_g, ln_b, *, bsz, seq, n_ctx, ctx_row, alpha, tm):
    nq = ATT_HEADS * HEAD_DIM
    nk = ATT_KV_HEADS * HEAD_DIM
    wq, wk, wv = w_in[:, :nq], w_in[:, nq:nq + nk], w_in[:, nq + nk:]
    wk, wv = _dup_heads(wk), _dup_heads(wv)
    w_cat = jnp.concatenate([wq, wk, wv], axis=1).astype(BF16)
    nrot = nq + wk.shape[1]
    tables = _rope_tables(seq)
    qkv_l, rot_l = _att_inproj_call(h_lat, modt, w_cat, tables, seq=seq, ctx_row=None, nq=nq, nrot=nrot, tm=tm)
    (qkv_c,) = _att_inproj_call(h_ctx, modt, w_cat, None, seq=n_ctx, ctx_row=ctx_row, nq=nq, nrot=nrot,
                                tm=min(tm, h_ctx.shape[0]))
    o_l = _attn_call(sink, qkv_l, rot_l, qkv_c, bsz=bsz, seq=seq, n_ctx=n_ctx, window=True)
    o_c = _attn_call(sink, qkv_c, None, qkv_c, bsz=bsz, seq=n_ctx, n_ctx=n_ctx, window=False)
    w_o = w_out.astype(BF16)
    h_lat = _outproj_ln_call(o_l, w_o, h_lat, modt, ln_g, ln_b, alpha=alpha, seq=seq, ctx_row=None, tm=tm)
    h_ctx = _outproj_ln_call(o_c, w_o, h_ctx, modt, ln_g, ln_b, alpha=alpha, seq=n_ctx, ctx_row=ctx_row,
                             tm=min(tm, h_ctx.shape[0]))
    return h_lat, h_ctx


def _ssm_layer(h_lat, h_ctx, modt, w_in, conv_w, conv_b, dt_bias, a_log, d_skip, norm_g, w_out, ln_g, ln_b,
               *, bsz, seq, n_ctx, ctx_row, alpha, tm):
    d_inner = w_out.shape[0]
    heads = d_inner // SSM_HEAD_DIM
    gcols = d_inner // SSM_GROUPS
    hpg = heads // SSM_GROUPS
    conv_dim = d_inner + 2 * SSM_GROUPS * SSM_STATE
    n_main = d_inner + conv_dim
    assert hpg == HEADS_PER_GROUP and gcols == SSM_GROUPS * SSM_STATE
    w_main = w_in[:, :n_main].astype(BF16)

    def regroup(v):
        lead = v.shape[:-2]
        v = v.reshape(lead + (2, SSM_GROUPS, hpg))
        v = jnp.moveaxis(v, -3, -2).reshape(lead + (SSM_GROUPS, 2 * hpg))
        return v

    w_dt = regroup(w_in[:, n_main:].reshape(-1, 2, heads)).reshape(-1, 2 * heads)
    w_dt = jnp.pad(w_dt, ((0, 0), (0, LANES - 2 * heads))).astype(BF16)
    dtb = jnp.pad(regroup(dt_bias).reshape(1, 2 * heads), ((0, 0), (0, LANES - 2 * heads)))
    a_log_row = jnp.pad(regroup(a_log).reshape(1, 2 * heads), ((0, 0), (0, LANES - 2 * heads)))
    d_row = jnp.repeat(d_skip, SSM_HEAD_DIM).reshape(1, d_inner)
    ng = norm_g.reshape(1, d_inner)
    ntile_conv = conv_dim // gcols
    cw = jnp.pad(conv_w, ((0, 8 - SSM_CONV_W), (0, 0))).reshape(8, ntile_conv, gcols).transpose(1, 0, 2)
    cb = conv_b.reshape(ntile_conv, 1, gcols)

    n_gate = d_inner // gcols
    gate_l, xbc_l, dt_l = _ssm_inproj_call(h_lat, modt, w_main, w_dt, dtb, cw, cb, n_dt=2 * heads,
                                           n_gate_tiles=n_gate, rows=seq, seq=seq, ctx_row=None, tile=gcols)
    ctx_rows = n_ctx * max(1, min(bsz, tm // n_ctx))
    while (bsz * n_ctx) % ctx_rows:
        ctx_rows -= n_ctx
    _, xbc_c, dt_c = _ssm_inproj_call(h_ctx, modt, w_main, w_dt, dtb, cw, cb, n_dt=2 * heads,
                                      n_gate_tiles=n_gate, rows=ctx_rows, seq=n_ctx, ctx_row=ctx_row, tile=gcols)
    y = _ssd_call(a_log_row, d_row, ng, xbc_c, dt_c, gate_l, xbc_l, dt_l, bsz=bsz, n_ctx=n_ctx, seq=seq)
    return _outproj_ln_call(y, w_out.astype(BF16), h_lat, modt, ln_g, ln_b, alpha=alpha, seq=seq, ctx_row=None, tm=tm)


def kernel(x, c, ctx, c_ctx, w_mod, b_mod, ln_mix_g, ln_mix_b, ln_ff_g, ln_ff_b, att_w_in, att_w_out, att_sink,
           ssm_w_in, ssm_conv_w, ssm_conv_b, ssm_dt_bias, ssm_a_log, ssm_d, ssm_norm_g, ssm_w_out, ff_w1, ff_w2):
    bsz, seq, d = x.shape
    n_ctx = ctx.shape[1]
    depth = w_mod.shape[0]
    alpha = (2.0 * depth) ** 0.25
    tm = 1024 if seq % 1024 == 0 and (bsz * n_ctx) % 1024 == 0 else 512
    assert seq % tm == 0 and (bsz * n_ctx) % min(tm, bsz * n_ctx) == 0 and seq % PERM_BLOCK == 0
    ctx_row = bsz
    rows = -(-(bsz + 1) // MOD_ROWS_ALIGN) * MOD_ROWS_ALIGN
    cc = jnp.concatenate([c, c_ctx[None, :], jnp.zeros((rows - bsz - 1, d), F32)], axis=0)
    mod = _mod_call(cc, w_mod, b_mod)
    h_lat = x.reshape(bsz * seq, d)
    h_ctx = ctx.reshape(bsz * n_ctx, d)
    for i in range(depth):
        last = i == depth - 1
        j = i // N_MIXERS
        modt = mod[i].reshape(rows * N_MOD, 1, d)
        if i % N_MIXERS == 0:
            h_lat, h_ctx_mix = _attention_layer(h_lat, h_ctx, modt, att_w_in[j], att_w_out[j], att_sink[j],
                                                ln_mix_g[i], ln_mix_b[i], bsz=bsz, seq=seq, n_ctx=n_ctx,
                                                ctx_row=ctx_row, alpha=alpha, tm=tm)
        else:
            if not last:
                raise NotImplementedError("context outputs of an SSD layer are only needed when it is not last")
            h_lat = _ssm_layer(h_lat, h_ctx, modt, ssm_w_in[j], ssm_conv_w[j], ssm_conv_b[j], ssm_dt_bias[j],
                               ssm_a_log[j], ssm_d[j], ssm_norm_g[j], ssm_w_out[j], ln_mix_g[i], ln_mix_b[i],
                               bsz=bsz, seq=seq, n_ctx=n_ctx, ctx_row=ctx_row, alpha=alpha, tm=tm)
            h_ctx_mix = None
        w1 = ff_w1[i].astype(BF16)
        w2 = ff_w2[i].astype(BF16)
        h_lat = _mlp_ln_call(h_lat, modt, w1, w2, ln_ff_g[i], ln_ff_b[i], alpha=alpha, seq=seq, ctx_row=None,
                             tm=tm, tf=1024)
        if not last:
            h_ctx = _mlp_ln_call(h_ctx_mix, modt, w1, w2, ln_ff_g[i], ln_ff_b[i], alpha=alpha, seq=n_ctx,
                                 ctx_row=ctx_row, tm=min(tm, h_ctx_mix.shape[0]), tf=1024)
    return h_lat.reshape(bsz, seq, d)
```
